```python
import math
import jax, jax.numpy as jnp
from jax import lax
import numpy as np

D_MODEL = 1024
BATCH = 8
SEQ = 4096
DEPTH = 2

CHUNK = 64
EPS = 1e-6
D_MLSTM = D_MODEL // 2
D_RGLRU = D_MODEL // 2
D_SSD = D_MODEL // 2
D_MIX = D_MLSTM + D_RGLRU + D_SSD
MLSTM_HEADS = 4
MLSTM_HEAD_DIM = D_MLSTM // MLSTM_HEADS
RGLRU_BLOCKS = 8
RGLRU_BLOCK_DIM = D_RGLRU // RGLRU_BLOCKS
RGLRU_C = 8.0
SSD_HEAD_DIM = 64
SSD_HEADS = D_SSD // SSD_HEAD_DIM
SSD_GROUPS = 2
SSD_STATE = 128
SSD_CONV_CH = D_SSD + 2 * SSD_GROUPS * SSD_STATE
CONV_WIDTH = 4
FFN_CONV_WIDTH = 3
D_FF = 2816
IN_SIZES = (D_MLSTM, D_MLSTM, MLSTM_HEADS, MLSTM_HEADS, D_RGLRU, D_RGLRU, D_SSD, SSD_CONV_CH, SSD_HEADS)
D_IN = D_MLSTM * 2 + MLSTM_HEADS * 2 + D_RGLRU * 2 + D_SSD + SSD_CONV_CH + SSD_HEADS

kernel_name = "hymba_mlstm_rglru_ssd_convffn_sandwich"


def rms_norm(x, g):
    xf = x.astype(jnp.float32)
    y = xf * lax.rsqrt(jnp.mean(xf * xf, axis=-1, keepdims=True) + EPS)
    return (y * g.astype(jnp.float32)).astype(x.dtype)


def causal_dwconv(x, w, b):
    K = w.shape[0]
    S = x.shape[1]
    xp = jnp.pad(x, ((0, 0), (K - 1, 0), (0, 0)))
    out = b
    for k in range(K):
        out = out + xp[:, k:k + S] * w[k]
    return out


def split_cols(p):
    idx = []
    acc = 0
    for s in IN_SIZES[:-1]:
        acc += s
        idx.append(acc)
    return jnp.split(p, idx, axis=-1)


def mlstm_group(x_m, o_pre, i_pre, f_pre, conv_w, conv_b, w_q, w_k, w_v, b_i, b_f, g_norm):
    Bsz, S, _ = x_m.shape
    H, dh, L = MLSTM_HEADS, MLSTM_HEAD_DIM, CHUNK
    NC = S // L
    f32 = jnp.float32
    x_c = jax.nn.silu(causal_dwconv(x_m, conv_w, conv_b))
    xc_h = x_c.reshape(Bsz, S, H, dh)
    xm_h = x_m.reshape(Bsz, S, H, dh)
    q = jnp.einsum('bshd,hde->bhse', xc_h, w_q).astype(f32).reshape(Bsz, H, NC, L, dh)
    k = (jnp.einsum('bshd,hde->bhse', xc_h, w_k).astype(f32) * (dh ** -0.5)).reshape(Bsz, H, NC, L, dh)
    v = jnp.einsum('bshd,hde->bhse', xm_h, w_v).astype(f32).reshape(Bsz, H, NC, L, dh)
    log_i = jnp.transpose((i_pre + b_i).astype(f32), (0, 2, 1)).reshape(Bsz, H, NC, L)
    log_f = jax.nn.log_sigmoid(jnp.transpose((f_pre + b_f).astype(f32), (0, 2, 1))).reshape(Bsz, H, NC, L)
    bcum = jnp.cumsum(log_f, axis=-1)
    b_tot = bcum[..., -1]
    causal = jnp.tril(jnp.ones((L, L), dtype=bool))
    d_intra = bcum[..., :, None] - bcum[..., None, :] + log_i[..., None, :]
    d_intra = jnp.where(causal, d_intra, -jnp.inf)
    w_state = b_tot[..., None] - bcum + log_i
    m_loc = jnp.max(w_state, axis=-1)
    ek = jnp.exp(w_state - m_loc[..., None])[..., None] * k
    c_loc = jnp.einsum('bhcld,bhcle->bhcde', ek, v)
    n_loc = jnp.sum(ek, axis=-2)

    def step(carry, inp):
        c_prev, n_prev, m_prev = carry
        c_l, n_l, m_l, bt = inp
        m_new = jnp.maximum(bt + m_prev, m_l)
        s_prev = jnp.exp(bt + m_prev - m_new)
        s_loc = jnp.exp(m_l - m_new)
        c_new = s_prev[..., None, None] * c_prev + s_loc[..., None, None] * c_l
        n_new = s_prev[..., None] * n_prev + s_loc[..., None] * n_l
        return (c_new, n_new, m_new), (c_prev, n_prev, m_prev)

    init = (jnp.zeros((Bsz, H, dh, dh), f32), jnp.zeros((Bsz, H, dh), f32), jnp.zeros((Bsz, H), f32))
    xs = (jnp.moveaxis(c_loc, 2, 0), jnp.moveaxis(n_loc, 2, 0), jnp.moveaxis(m_loc, 2, 0), jnp.moveaxis(b_tot, 2, 0))
    _, (c_st, n_st, m_st) = lax.scan(step, init, xs)
    c_st = jnp.moveaxis(c_st, 0, 2)
    n_st = jnp.moveaxis(n_st, 0, 2)
    m_st = jnp.moveaxis(m_st, 0, 2)
    inter_log = bcum + m_st[..., None]
    m_t = jnp.maximum(inter_log, jnp.max(d_intra, axis=-1))
    e_inter = jnp.exp(inter_log - m_t)
    scores = jnp.einsum('bhctd,bhcsd->bhcts', q, k) * jnp.exp(d_intra - m_t[..., None])
    num = jnp.einsum('bhcts,bhcse->bhcte', scores, v) + e_inter[..., None] * jnp.einsum('bhctd,bhcde->bhcte', q, c_st)
    den = jnp.sum(scores, axis=-1) + e_inter * jnp.einsum('bhctd,bhcd->bhct', q, n_st)
    h = num / jnp.maximum(jnp.abs(den), jnp.exp(-m_t))[..., None]
    h = jnp.transpose(h.reshape(Bsz, H, S, dh), (0, 2, 1, 3))
    h = jax.nn.sigmoid(o_pre.astype(f32)).reshape(Bsz, S, H, dh) * h
    h = h * lax.rsqrt(jnp.mean(h * h, axis=-1, keepdims=True) + EPS)
    h = h * g_norm.astype(f32).reshape(H, dh)
    return h.reshape(Bsz, S, D_MLSTM).astype(x_m.dtype)


def rglru_group(x_r, y_r, conv_w, conv_b, w_a, b_a, w_x, b_x, lam):
    Bsz, S, _ = x_r.shape
    f32 = jnp.float32
    xc = causal_dwconv(x_r, conv_w, conv_b)
    xb = xc.reshape(Bsz, S, RGLRU_BLOCKS, RGLRU_BLOCK_DIM)
    r = jax.nn.sigmoid(jnp.einsum('bsnd,nde->bsne', xb, w_a).reshape(Bsz, S, D_RGLRU) + b_a)
    i = jax.nn.sigmoid(jnp.einsum('bsnd,nde->bsne', xb, w_x).reshape(Bsz, S, D_RGLRU) + b_x)
    log_a = -RGLRU_C * r.astype(f32) * jax.nn.softplus(-lam.astype(f32))
    a = jnp.exp(log_a)
    u = jnp.sqrt(-jnp.expm1(2.0 * log_a)) * (i * xc).astype(f32)

    def combine(left, right):
        a1, b1 = left
        a2, b2 = right
        return a2 * a1, a2 * b1 + b2

    _, h = lax.associative_scan(combine, (a, u), axis=1)
    return h.astype(x_r.dtype) * jax.nn.gelu(y_r)


def ssd_group(z, xbc, dt_raw, conv_w, conv_b, dt_bias, a_log, d_skip, g_norm):
    Bsz, S, _ = z.shape
    H, P, G, N, L = SSD_HEADS, SSD_HEAD_DIM, SSD_GROUPS, SSD_STATE, CHUNK
    NC = S // L
    f32 = jnp.float32
    xbc = jax.nn.silu(causal_dwconv(xbc, conv_w, conv_b))
    xs, Bm, Cm = jnp.split(xbc, [D_SSD, D_SSD + G * N], axis=-1)
    xs = xs.reshape(Bsz, S, H, P)
    Bh = jnp.repeat(Bm.reshape(Bsz, S, G, N), H // G, axis=2)
    Ch = jnp.repeat(Cm.reshape(Bsz, S, G, N), H // G, axis=2)
    dt = jax.nn.softplus(dt_raw.astype(f32) + dt_bias.astype(f32))
    a = dt * (-jnp.exp(a_log.astype(f32)))
    x_c = xs.reshape(Bsz, NC, L, H, P)
    Bc = Bh.reshape(Bsz, NC, L, H, N)
    Cc = Ch.reshape(Bsz, NC, L, H, N)
    xdt = x_c * dt.reshape(Bsz, NC, L, H)[..., None]
    a_cum = jnp.cumsum(jnp.transpose(a.reshape(Bsz, NC, L, H), (0, 3, 1, 2)), axis=-1)
    causal = jnp.tril(jnp.ones((L, L), dtype=bool))
    seg = jnp.where(causal, a_cum[..., :, None] - a_cum[..., None, :], -jnp.inf)
    cb = jnp.einsum('bclhn,bcshn->bhcls', Cc, Bc) * jnp.exp(seg)
    y_diag = jnp.einsum('bhcls,bcshp->bclhp', cb, xdt)
    decay_states = jnp.exp(a_cum[..., -1:] - a_cum)
    states = jnp.einsum('bclhn,bhcl,bclhp->bchpn', Bc, decay_states, xdt)
    chunk_decay = jnp.exp(a_cum[..., -1])

    def step(carry, inp):
        st, dec = inp
        return dec[..., None, None] * carry + st, carry

    init = jnp.zeros((Bsz, H, P, N), states.dtype)
    _, s_start = lax.scan(step, init, (jnp.moveaxis(states, 1, 0), jnp.moveaxis(chunk_decay, 2, 0)))
    s_start = jnp.moveaxis(s_start, 0, 1)
    y_off = jnp.einsum('bclhn,bchpn,bhcl->bclhp', Cc, s_start, jnp.exp(a_cum))
    y = (y_diag + y_off).reshape(Bsz, S, H, P) + xs * d_skip[:, None]
    y = y.reshape(Bsz, S, D_SSD).astype(z.dtype)
    return rms_norm(y * jax.nn.silu(z), g_norm)


def conv_ffn(x, w_up, conv_w, conv_b, w_down):
    u = causal_dwconv(x @ w_up, conv_w, conv_b)
    g, v = jnp.split(u, 2, axis=-1)
    return (jax.nn.gelu(g) * v) @ w_down


def setup_inputs(seed: int = 0) -> dict:
    key = jax.random.key(seed)
    ks = jax.random.split(key, 40)
    f32 = jnp.float32

    def nrm(k, shape, scale):
        return jax.random.normal(k, shape, f32) * scale

    def gain(k, n):
        return 1.0 + 0.02 * jax.random.normal(k, (DEPTH, n), f32)

    u_a = jax.random.uniform(ks[20], (DEPTH, D_RGLRU), f32, minval=0.9, maxval=0.999)
    a_base = u_a ** (1.0 / RGLRU_C)
    dt0 = jnp.exp(jax.random.uniform(ks[24], (DEPTH, SSD_HEADS), f32, minval=math.log(1e-3), maxval=math.log(1e-1)))
    return {
        "x": jax.random.normal(ks[0], (BATCH, SEQ, D_MODEL), f32),
        "norm_mix_pre": gain(ks[1], D_MODEL),
        "norm_mix_post": gain(ks[2], D_MODEL),
        "norm_ffn_pre": gain(ks[3], D_MODEL),
        "norm_ffn_post": gain(ks[4], D_MODEL),
        "w_in": nrm(ks[5], (DEPTH, D_MODEL, D_IN), D_MODEL ** -0.5),
        "conv_m_w": nrm(ks[6], (DEPTH, CONV_WIDTH, D_MLSTM), CONV_WIDTH ** -0.5),
        "conv_m_b": nrm(ks[7], (DEPTH, D_MLSTM), 0.02),
        "w_q_m": nrm(ks[8], (DEPTH, MLSTM_HEADS, MLSTM_HEAD_DIM, MLSTM_HEAD_DIM), MLSTM_HEAD_DIM ** -0.5),
        "w_k_m": nrm(ks[9], (DEPTH, MLSTM_HEADS, MLSTM_HEAD_DIM, MLSTM_HEAD_DIM), MLSTM_HEAD_DIM ** -0.5),
        "w_v_m": nrm(ks[10], (DEPTH, MLSTM_HEADS, MLSTM_HEAD_DIM, MLSTM_HEAD_DIM), MLSTM_HEAD_DIM ** -0.5),
        "b_i_m": nrm(ks[11], (DEPTH, MLSTM_HEADS), 0.1),
        "b_f_m": jnp.linspace(3.0, 6.0, MLSTM_HEADS, dtype=f32)[None, :] + nrm(ks[12], (DEPTH, MLSTM_HEADS), 0.1),
        "norm_m": gain(ks[13], D_MLSTM),
        "conv_r_w": nrm(ks[14], (DEPTH, CONV_WIDTH, D_RGLRU), CONV_WIDTH ** -0.5),
        "conv_r_b": nrm(ks[15], (DEPTH, D_RGLRU), 0.02),
        "w_a_r": nrm(ks[16], (DEPTH, RGLRU_BLOCKS, RGLRU_BLOCK_DIM, RGLRU_BLOCK_DIM), RGLRU_BLOCK_DIM ** -0.5),
        "b_a_r": nrm(ks[17], (DEPTH, D_RGLRU), 0.02),
        "w_x_r": nrm(ks[18], (DEPTH, RGLRU_BLOCKS, RGLRU_BLOCK_DIM, RGLRU_BLOCK_DIM), RGLRU_BLOCK_DIM ** -0.5),
        "b_x_r": nrm(ks[19], (DEPTH, D_RGLRU), 0.02),
        "lam_r": jnp.log(a_base) - jnp.log1p(-a_base),
        "conv_s_w": nrm(ks[21], (DEPTH, CONV_WIDTH, SSD_CONV_CH), CONV_WIDTH ** -0.5),
        "conv_s_b": nrm(ks[22], (DEPTH, SSD_CONV_CH), 0.02),
        "dt_bias_s": dt0 + jnp.log(-jnp.expm1(-dt0)),
        "a_log_s": jnp.log(jax.random.uniform(ks[25], (DEPTH, SSD_HEADS), f32, minval=1.0, maxval=16.0)),
        "d_skip_s": 1.0 + nrm(ks[26], (DEPTH, SSD_HEADS), 0.1),
        "norm_s": gain(ks[27], D_SSD),
        "w_out": nrm(ks[28], (DEPTH, D_MIX, D_MODEL), D_MIX ** -0.5),
        "w_up": nrm(ks[29], (DEPTH, D_MODEL, 2 * D_FF), D_MODEL ** -0.5),
        "conv_f_w": nrm(ks[30], (DEPTH, FFN_CONV_WIDTH, 2 * D_FF), FFN_CONV_WIDTH ** -0.5),
        "conv_f_b": nrm(ks[31], (DEPTH, 2 * D_FF), 0.02),
        "w_down": nrm(ks[32], (DEPTH, D_FF, D_MODEL), D_FF ** -0.5),
    }


def reference(x, norm_mix_pre, norm_mix_post, norm_ffn_pre, norm_ffn_post, w_in,
              conv_m_w, conv_m_b, w_q_m, w_k_m, w_v_m, b_i_m, b_f_m, norm_m,
              conv_r_w, conv_r_b, w_a_r, b_a_r, w_x_r, b_x_r, lam_r,
              conv_s_w, conv_s_b, dt_bias_s, a_log_s, d_skip_s, norm_s,
              w_out, w_up, conv_f_w, conv_f_b, w_down):
    for l in range(DEPTH):
        h = rms_norm(x, norm_mix_pre[l])
        xm, om, im, fm, xr, yr, zs, xbcs, dts = split_cols(h @ w_in[l])
        y_m = mlstm_group(xm, om, im, fm, conv_m_w[l], conv_m_b[l], w_q_m[l], w_k_m[l], w_v_m[l],
                          b_i_m[l], b_f_m[l], norm_m[l])
        y_r = rglru_group(xr, yr, conv_r_w[l], conv_r_b[l], w_a_r[l], b_a_r[l], w_x_r[l], b_x_r[l], lam_r[l])
        y_s = ssd_group(zs, xbcs, dts, conv_s_w[l], conv_s_b[l], dt_bias_s[l], a_log_s[l], d_skip_s[l], norm_s[l])
        mix = jnp.concatenate([y_m, y_r, y_s], axis=-1) @ w_out[l]
        x = x + rms_norm(mix, norm_mix_post[l])
        h = rms_norm(x, norm_ffn_pre[l])
        x = x + rms_norm(conv_ffn(h, w_up[l], conv_f_w[l], conv_f_b[l], w_down[l]), norm_ffn_post[l])
    return x
```

```python
import functools
import math

import jax
import jax.numpy as jnp
from jax import lax
from jax.experimental import pallas as pl
from jax.experimental.pallas import tpu as pltpu

F32 = jnp.float32
BF16 = jnp.bfloat16
EPS = 1e-6

LANES = 128
HALO = 16
VMEM_LIMIT = 56 * 1024 * 1024

MLSTM_HEADS = 4
RGLRU_BLOCKS = 8
RGLRU_C = 8.0
SSD_HEAD_DIM = 64
SSD_GROUPS = 2
SSD_STATE = 128
GATE_I, GATE_F, GATE_DT = 0, 4, 8

ROWS_PROJ = 512
ROWS_FFN = 256
CHUNK = 128
ROWS_LRU = 256
FFN_COLS = 256


def _rms(x, g):
    return x * lax.rsqrt(jnp.mean(x * x, axis=-1, keepdims=True) + EPS) * g


def _softplus(x):
    return jnp.maximum(x, 0.0) + jnp.log1p(jnp.exp(-jnp.abs(x)))


def _causal_conv(x, halo, w, b):
    k = w.shape[0]
    xe = jnp.concatenate([halo, x], axis=0)
    out = b + w[k - 1:k] * x
    for j in range(1, k):
        out = out + w[k - 1 - j:k - j] * pltpu.roll(xe, j, 0)[HALO:]
    return out


def _lane_pair(col_lo, col_hi, n, width):
    lane = lax.broadcasted_iota(jnp.int32, (n, width), 1)
    return jnp.where(lane < width // 2, col_lo, col_hi)


def _in_proj_kernel(x_ref, g_ref, w_ref, pm_ref, pg_ref, *, n_main, col_chunk):
    h = _rms(x_ref[...], g_ref[...]).astype(BF16)
    for c0 in range(0, n_main, col_chunk):
        pm_ref[:, c0:c0 + col_chunk] = jnp.dot(
            h, w_ref[:, c0:c0 + col_chunk], preferred_element_type=F32).astype(BF16)
    pg_ref[...] = jnp.dot(h, w_ref[:, n_main:], preferred_element_type=F32)


def _in_proj(x2, g, w, n_main):
    t, d = x2.shape
    n_all = w.shape[1]
    tm = ROWS_PROJ
    return pl.pallas_call(
        functools.partial(_in_proj_kernel, n_main=n_main, col_chunk=512),
        grid=(t // tm,),
        in_specs=[
            pl.BlockSpec((tm, d), lambda i: (i, 0)),
            pl.BlockSpec((1, d), lambda i: (0, 0)),
            pl.BlockSpec((d, n_all), lambda i: (0, 0)),
        ],
        out_specs=[
            pl.BlockSpec((tm, n_main), lambda i: (i, 0)),
            pl.BlockSpec((tm, n_all - n_main), lambda i: (i, 0)),
        ],
        out_shape=[
            jax.ShapeDtypeStruct((t, n_main), BF16),
            jax.ShapeDtypeStruct((t, n_all - n_main), F32),
        ],
        compiler_params=pltpu.CompilerParams(
            dimension_semantics=("arbitrary",), vmem_limit_bytes=VMEM_LIMIT),
        name="in_proj",
    )(x2, g, w)


def _mlstm_kernel(xm_ref, xh_ref, om_ref, gate_ref, cw_ref, cb_ref, wq_ref, wk_ref, wv_ref,
                  gb_ref, gn_ref, y_ref, c_ref, m_ref, *, n, heads, dh):
    t = pl.program_id(1)

    @pl.when(t == 0)
    def _init():
        c_ref[...] = jnp.zeros_like(c_ref)
        m_ref[...] = jnp.zeros_like(m_ref)

    xm = xm_ref[...].astype(F32)
    halo = jnp.where(t > 0, xh_ref[...].astype(F32), 0.0)
    xc = _causal_conv(xm, halo, cw_ref[...], cb_ref[...])
    xc = xc * jax.nn.sigmoid(xc)

    g = gate_ref[...] + gb_ref[...]
    lf = -_softplus(-g)
    row = lax.broadcasted_iota(jnp.int32, (n, n), 0)
    col = lax.broadcasted_iota(jnp.int32, (n, n), 1)
    causal = row >= col
    cs = jnp.dot(causal.astype(F32), lf, precision=lax.Precision.HIGHEST,
                 preferred_element_type=F32)
    cs_t = cs.T
    g_t = g.T
    ones = jnp.ones((n, dh), BF16)
    scale = dh ** -0.5

    for h in range(heads):
        sl = slice(h * dh, (h + 1) * dh)
        xc_h = xc[:, sl].astype(BF16)
        q = jnp.dot(xc_h, wq_ref[h], preferred_element_type=F32).astype(BF16)
        k = jnp.dot(xc_h, wk_ref[h], preferred_element_type=F32) * scale
        v = jnp.dot(xm_ref[:, sl], wv_ref[h], preferred_element_type=F32).astype(BF16)
        v_aug = jnp.concatenate([v, ones], axis=1)
        k_bf = k.astype(BF16)

        bcol = cs[:, GATE_F + h:GATE_F + h + 1]
        brow = cs_t[GATE_F + h:GATE_F + h + 1, :]
        icol = g[:, GATE_I + h:GATE_I + h + 1]
        irow = g_t[GATE_I + h:GATE_I + h + 1, :]
        btot = cs[n - 1:n, GATE_F + h:GATE_F + h + 1]
        m_prev = m_ref[h:h + 1, 0:1]

        d = jnp.where(causal, bcol - brow + irow, -jnp.inf)
        inter = bcol + m_prev
        m_t = jnp.maximum(inter, jnp.max(d, axis=1, keepdims=True))
        e_inter = jnp.exp(inter - m_t)
        scores = lax.dot_general(q, k_bf, (((1,), (1,)), ((), ())),
                                 preferred_element_type=F32) * jnp.exp(d - m_t)
        c_prev = c_ref[h]
        comb = (jnp.dot(scores.astype(BF16), v_aug, preferred_element_type=F32)
                + e_inter * jnp.dot(q, c_prev.astype(BF16), preferred_element_type=F32))
        num = comb[:, :dh]
        den = comb[:, dh:]
        hh = num / jnp.maximum(jnp.abs(den), jnp.exp(-m_t))

        wst = btot - bcol + icol
        m_loc = jnp.max(wst, axis=0, keepdims=True)
        ek = (jnp.exp(wst - m_loc) * k).astype(BF16)
        c_loc = lax.dot_general(ek, v_aug, (((0,), (0,)), ((), ())), preferred_element_type=F32)
        m_new = jnp.maximum(btot + m_prev, m_loc)
        c_ref[h] = jnp.exp(btot + m_prev - m_new) * c_prev + jnp.exp(m_loc - m_new) * c_loc
        m_ref[h:h + 1, :] = jnp.broadcast_to(m_new, (1, LANES))

        o = jax.nn.sigmoid(om_ref[:, sl].astype(F32)) * hh
        y_ref[:, sl] = _rms(o, gn_ref[:, sl]).astype(y_ref.dtype)


def _mlstm(pm, pg, cw, cb, wq, wk, wv, gb, gn, *, batch, seq):
    n = CHUNK
    heads = MLSTM_HEADS
    d = cw.shape[1]
    dh = d // heads
    nt = seq // n
    rows = lambda b, t: b * nt + t
    halo = lambda b, t: jnp.maximum((b * seq + t * n) // HALO - 1, 0)
    const2 = lambda b, t: (0, 0)
    const3 = lambda b, t: (0, 0, 0)
    return pl.pallas_call(
        functools.partial(_mlstm_kernel, n=n, heads=heads, dh=dh),
        grid=(batch, nt),
        in_specs=[
            pl.BlockSpec((n, d), lambda b, t: (rows(b, t), 0)),
            pl.BlockSpec((HALO, d), lambda b, t: (halo(b, t), 0)),
            pl.BlockSpec((n, d), lambda b, t: (rows(b, t), 1)),
            pl.BlockSpec((n, LANES), lambda b, t: (rows(b, t), 0)),
            pl.BlockSpec(cw.shape, const2),
            pl.BlockSpec(cb.shape, const2),
            pl.BlockSpec(wq.shape, const3),
            pl.BlockSpec(wk.shape, const3),
            pl.BlockSpec(wv.shape, const3),
            pl.BlockSpec(gb.shape, const2),
            pl.BlockSpec(gn.shape, const2),
        ],
        out_specs=pl.BlockSpec((n, d), lambda b, t: (rows(b, t), 0)),
        out_shape=jax.ShapeDtypeStruct((batch * seq, d), BF16),
        scratch_shapes=[pltpu.VMEM((heads, dh, 2 * dh), F32), pltpu.VMEM((8, LANES), F32)],
        compiler_params=pltpu.CompilerParams(
            dimension_semantics=("arbitrary", "arbitrary"), vmem_limit_bytes=VMEM_LIMIT),
        name="mlstm",
    )(pm, pm, pm, pg, cw, cb, wq, wk, wv, gb, gn)


def _rglru_kernel(xr_ref, xh_ref, yr_ref, cw_ref, cb_ref, wa_ref, wx_ref, ba_ref, bx_ref, lam_ref,
                  y_ref, h_ref, *, n, tile):
    t = pl.program_id(1)

    @pl.when(t == 0)
    def _init():
        h_ref[...] = jnp.zeros_like(h_ref)

    x = xr_ref[...].astype(F32)
    halo = jnp.where(t > 0, xh_ref[...].astype(F32), 0.0)
    xc = _causal_conv(x, halo, cw_ref[...], cb_ref[...])
    xc_bf = xc.astype(BF16)
    d = xc.shape[1]
    ra, ri = [], []
    for j in range(d // tile):
        blk = xc_bf[:, j * tile:(j + 1) * tile]
        ra.append(jnp.dot(blk, wa_ref[j], preferred_element_type=F32))
        ri.append(jnp.dot(blk, wx_ref[j], preferred_element_type=F32))
    r = jax.nn.sigmoid(jnp.concatenate(ra, axis=1) + ba_ref[...])
    i = jax.nn.sigmoid(jnp.concatenate(ri, axis=1) + bx_ref[...])
    log_a = (-RGLRU_C) * r * _softplus(-lam_ref[...])
    a = jnp.exp(log_a)
    u = jnp.sqrt(1.0 - a * a) * (i * xc)

    rmod = lax.broadcasted_iota(jnp.int32, (n, d), 0) & 7
    for s in (1, 2, 4):
        keep = rmod >= s
        u = jnp.where(keep, a * pltpu.roll(u, s, 0) + u, u)
        a = jnp.where(keep, a * pltpu.roll(a, s, 0), a)
    gel = jax.nn.gelu(yr_ref[...].astype(F32), approximate=True)
    carry = h_ref[...]
    for j in range(n // 8):
        rs = slice(j * 8, (j + 1) * 8)
        hj = a[rs] * carry + u[rs]
        y_ref[rs, :] = (hj * gel[rs]).astype(y_ref.dtype)
        carry = hj[7:8]
    h_ref[...] = carry


def _rglru(pm, cw, cb, wa, wx, ba, bx, lam, *, batch, seq, col0):
    n = ROWS_LRU
    d = cw.shape[1]
    nt = seq // n
    rows = lambda b, t: b * nt + t
    halo = lambda b, t: jnp.maximum((b * seq + t * n) // HALO - 1, 0)
    const2 = lambda b, t: (0, 0)
    const3 = lambda b, t: (0, 0, 0)
    return pl.pallas_call(
        functools.partial(_rglru_kernel, n=n, tile=wa.shape[1]),
        grid=(batch, nt),
        in_specs=[
            pl.BlockSpec((n, d), lambda b, t: (rows(b, t), col0)),
            pl.BlockSpec((HALO, d), lambda b, t: (halo(b, t), col0)),
            pl.BlockSpec((n, d), lambda b, t: (rows(b, t), col0 + 1)),
            pl.BlockSpec(cw.shape, const2),
            pl.BlockSpec(cb.shape, const2),
            pl.BlockSpec(wa.shape, const3),
            pl.BlockSpec(wx.shape, const3),
            pl.BlockSpec(ba.shape, const2),
            pl.BlockSpec(bx.shape, const2),
            pl.BlockSpec(lam.shape, const2),
        ],
        out_specs=pl.BlockSpec((n, d), lambda b, t: (rows(b, t), 0)),
        out_shape=jax.ShapeDtypeStruct((batch * seq, d), BF16),
        scratch_shapes=[pltpu.VMEM((1, d), F32)],
        compiler_params=pltpu.CompilerParams(
            dimension_semantics=("arbitrary", "arbitrary"), vmem_limit_bytes=VMEM_LIMIT),
        name="rglru",
    )(pm, pm, pm, cw, cb, wa, wx, ba, bx, lam)


def _ssd_kernel(z_ref, xs_ref, xsh_ref, bc_ref, bch_ref, gate_ref, cw_ref, cb_ref, gb_ref, alog_ref,
                dskip_ref, gn_ref, y_ref, s_ref, *, n, heads, hd, groups, ns):
    t = pl.program_id(1)

    @pl.when(t == 0)
    def _init():
        s_ref[...] = jnp.zeros_like(s_ref)

    d = heads * hd
    cw = cw_ref[...]
    cb = cb_ref[...]
    live = t > 0
    xs = _causal_conv(xs_ref[...].astype(F32), jnp.where(live, xsh_ref[...].astype(F32), 0.0),
                      cw[:, :d], cb[:, :d])
    xs = xs * jax.nn.sigmoid(xs)
    bc = _causal_conv(bc_ref[...].astype(F32), jnp.where(live, bch_ref[...].astype(F32), 0.0),
                      cw[:, d:], cb[:, d:])
    bc = bc * jax.nn.sigmoid(bc)

    dt = _softplus(gate_ref[...] + gb_ref[...])
    a = dt * (-jnp.exp(alog_ref[...]))
    row = lax.broadcasted_iota(jnp.int32, (n, n), 0)
    col = lax.broadcasted_iota(jnp.int32, (n, n), 1)
    causal = row >= col
    acs = jnp.dot(causal.astype(F32), a, precision=lax.Precision.HIGHEST,
                  preferred_element_type=F32)
    acs_t = acs.T

    pair = 2 * hd
    hpg = heads // groups
    lane = lax.broadcasted_iota(jnp.int32, (n, pair), 1)
    lo = lane < hd
    ys = []
    for grp in range(groups):
        b_g = bc[:, grp * ns:(grp + 1) * ns].astype(BF16)
        c_g = bc[:, (groups + grp) * ns:(groups + grp + 1) * ns].astype(BF16)
        cbm = lax.dot_general(c_g, b_g, (((1,), (1,)), ((), ())), preferred_element_type=F32)
        for pp in range(hpg // 2):
            h0 = grp * hpg + 2 * pp
            ps = slice(h0 * hd, h0 * hd + pair)
            cols = [GATE_DT + h0, GATE_DT + h0 + 1]
            acol = [acs[:, c:c + 1] for c in cols]
            arow = [acs_t[c:c + 1, :] for c in cols]
            atot = [acs[n - 1:n, c:c + 1] for c in cols]
            dt_p = _lane_pair(dt[:, cols[0]:cols[0] + 1], dt[:, cols[1]:cols[1] + 1], n, pair)
            xs_p = xs[:, ps]
            xdt = xs_p * dt_p
            ydiag = jnp.zeros((n, pair), F32)
            for e in range(2):
                dec = jnp.exp(jnp.where(causal, acol[e] - arow[e], -jnp.inf))
                half = jnp.where(lo if e == 0 else jnp.logical_not(lo), xdt, 0.0).astype(BF16)
                ydiag = ydiag + jnp.dot((cbm * dec).astype(BF16), half, preferred_element_type=F32)
            s_prev = s_ref[:, ps]
            yoff = (jnp.dot(c_g, s_prev.astype(BF16), preferred_element_type=F32)
                    * jnp.exp(_lane_pair(acol[0], acol[1], n, pair)))
            dst = jnp.exp(_lane_pair(atot[0] - acol[0], atot[1] - acol[1], n, pair))
            s_loc = lax.dot_general(b_g, (xdt * dst).astype(BF16), (((0,), (0,)), ((), ())),
                                    preferred_element_type=F32)
            s_ref[:, ps] = jnp.exp(_lane_pair(atot[0], atot[1], 1, pair)) * s_prev + s_loc
            ys.append(ydiag + yoff + xs_p * dskip_ref[:, ps])
    y = jnp.concatenate(ys, axis=1)
    z = z_ref[...].astype(F32)
    y_ref[...] = _rms(y * (z * jax.nn.sigmoid(z)), gn_ref[...]).astype(y_ref.dtype)


def _ssd(pm, pg, cw, cb, gb, alog, dskip, gn, *, batch, seq, col0):
    n = CHUNK
    d = gn.shape[1]
    heads = d // SSD_HEAD_DIM
    nt = seq // n
    rows = lambda b, t: b * nt + t
    halo = lambda b, t: jnp.maximum((b * seq + t * n) // HALO - 1, 0)
    const2 = lambda b, t: (0, 0)
    return pl.pallas_call(
        functools.partial(_ssd_kernel, n=n, heads=heads, hd=SSD_HEAD_DIM, groups=SSD_GROUPS,
                          ns=SSD_STATE),
        grid=(batch, nt),
        in_specs=[
            pl.BlockSpec((n, d), lambda b, t: (rows(b, t), col0)),
            pl.BlockSpec((n, d), lambda b, t: (rows(b, t), col0 + 1)),
            pl.BlockSpec((HALO, d), lambda b, t: (halo(b, t), col0 + 1)),
            pl.BlockSpec((n, d), lambda b, t: (rows(b, t), col0 + 2)),
            pl.BlockSpec((HALO, d), lambda b, t: (halo(b, t), col0 + 2)),
            pl.BlockSpec((n, LANES), lambda b, t: (rows(b, t), 0)),
            pl.BlockSpec(cw.shape, const2),
            pl.BlockSpec(cb.shape, const2),
            pl.BlockSpec(gb.shape, const2),
            pl.BlockSpec(alog.shape, const2),
            pl.BlockSpec(dskip.shape, const2),
            pl.BlockSpec(gn.shape, const2),
        ],
        out_specs=pl.BlockSpec((n, d), lambda b, t: (rows(b, t), 0)),
        out_shape=jax.ShapeDtypeStruct((batch * seq, d), BF16),
        scratch_shapes=[pltpu.VMEM((SSD_STATE, d), F32)],
        compiler_params=pltpu.CompilerParams(
            dimension_semantics=("arbitrary", "arbitrary"), vmem_limit_bytes=VMEM_LIMIT),
        name="ssd",
    )(pm, pm, pm, pm, pm, pg, cw, cb, gb, alog, dskip, gn)


def _out_proj_kernel(ym_ref, yr_ref, ys_ref, x_ref, wo_ref, gpost_ref, gpre_ref, x1_ref, h2_ref, *, d_grp):
    mix = (jnp.dot(ym_ref[...], wo_ref[0:d_grp, :], preferred_element_type=F32)
           + jnp.dot(yr_ref[...], wo_ref[d_grp:2 * d_grp, :], preferred_element_type=F32)
           + jnp.dot(ys_ref[...], wo_ref[2 * d_grp:3 * d_grp, :], preferred_element_type=F32))
    x1 = x_ref[...] + _rms(mix, gpost_ref[...])
    x1_ref[...] = x1
    h2_ref[...] = _rms(x1, gpre_ref[...]).astype(h2_ref.dtype)


def _out_proj(ym, yr, ys, x2, wo, gpost, gpre):
    t, d = x2.shape
    d_grp = ym.shape[1]
    tm = ROWS_PROJ
    rowblk = lambda i: (i, 0)
    const = lambda i: (0, 0)
    return pl.pallas_call(
        functools.partial(_out_proj_kernel, d_grp=d_grp),
        grid=(t // tm,),
        in_specs=[
            pl.BlockSpec((tm, d_grp), rowblk),
            pl.BlockSpec((tm, d_grp), rowblk),
            pl.BlockSpec((tm, d_grp), rowblk),
            pl.BlockSpec((tm, d), rowblk),
            pl.BlockSpec(wo.shape, const),
            pl.BlockSpec((1, d), const),
            pl.BlockSpec((1, d), const),
        ],
        out_specs=[pl.BlockSpec((tm, d), rowblk), pl.BlockSpec((tm, d), rowblk)],
        out_shape=[jax.ShapeDtypeStruct((t, d), F32), jax.ShapeDtypeStruct((t, d), BF16)],
        compiler_params=pltpu.CompilerParams(
            dimension_semantics=("arbitrary",), vmem_limit_bytes=VMEM_LIMIT),
        name="out_proj",
    )(ym, yr, ys, x2, wo, gpost, gpre)


def _ffn_kernel(h_ref, hh_ref, x_ref, wup_ref, cw_ref, cb_ref, wdn_ref, gpost_ref, o_ref, acc_ref,
                *, d_ff, cols):
    t = pl.program_id(1)
    halo = jnp.where(t > 0, hh_ref[...], jnp.zeros_like(hh_ref))
    he = jnp.concatenate([halo, h_ref[...]], axis=0)
    acc_ref[...] = jnp.zeros_like(acc_ref)
    for c0 in range(0, d_ff, cols):
        branch = []
        for base in (c0, d_ff + c0):
            cs = slice(base, base + cols)
            u = jnp.dot(he, wup_ref[:, cs], preferred_element_type=F32)
            w = cw_ref[:, cs]
            branch.append(cb_ref[:, cs] + w[2:3] * u[HALO:] + w[1:2] * pltpu.roll(u, 1, 0)[HALO:]
                          + w[0:1] * pltpu.roll(u, 2, 0)[HALO:])
        act = (jax.nn.gelu(branch[0], approximate=True) * branch[1]).astype(BF16)
        acc_ref[...] += jnp.dot(act, wdn_ref[c0:c0 + cols, :], preferred_element_type=F32)
    o_ref[...] = x_ref[...] + _rms(acc_ref[...], gpost_ref[...])


def _ffn(h2, x1, wup, cw, cb, wdn, gpost, *, batch, seq):
    t, d = x1.shape
    d_ff = wdn.shape[0]
    tm = ROWS_FFN
    nt = seq // tm
    rows = lambda b, i: (b * nt + i, 0)
    halo = lambda b, i: (jnp.maximum((b * seq + i * tm) // HALO - 1, 0), 0)
    const = lambda b, i: (0, 0)
    return pl.pallas_call(
        functools.partial(_ffn_kernel, d_ff=d_ff, cols=FFN_COLS),
        grid=(batch, nt),
        in_specs=[
            pl.BlockSpec((tm, d), rows),
            pl.BlockSpec((HALO, d), halo),
            pl.BlockSpec((tm, d), rows),
            pl.BlockSpec(wup.shape, const),
            pl.BlockSpec(cw.shape, const),
            pl.BlockSpec(cb.shape, const),
            pl.BlockSpec(wdn.shape, const),
            pl.BlockSpec((1, d), const),
        ],
        out_specs=pl.BlockSpec((tm, d), rows),
        out_shape=jax.ShapeDtypeStruct((t, d), F32),
        scratch_shapes=[pltpu.VMEM((tm, d), F32)],
        compiler_params=pltpu.CompilerParams(
            dimension_semantics=("arbitrary", "arbitrary"), vmem_limit_bytes=VMEM_LIMIT),
        name="ffn",
    )(h2, h2, x1, wup, cw, cb, wdn, gpost)


def _block_diag(w, per_tile):
    nb, e, _ = w.shape
    w = w.reshape(nb // per_tile, per_tile, e, e)
    eye = jnp.eye(per_tile, dtype=w.dtype)
    out = jnp.einsum("tpij,pq->tpiqj", w, eye)
    return out.reshape(nb // per_tile, per_tile * e, per_tile * e)


def kernel(x, norm_mix_pre, norm_mix_post, norm_ffn_pre, norm_ffn_post, w_in, conv_m_w, conv_m_b, w_q_m, w_k_m, w_v_m, b_i_m, b_f_m, norm_m, conv_r_w, conv_r_b, w_a_r, b_a_r, w_x_r, b_x_r, lam_r, conv_s_w, conv_s_b, dt_bias_s, a_log_s, d_skip_s, norm_s, w_out, w_up, conv_f_w, conv_f_b, w_down):
    batch, seq, d = x.shape
    depth = w_in.shape[0]
    dm = conv_m_w.shape[2]
    dr = conv_r_w.shape[2]
    ds = norm_s.shape[1]
    dconv = conv_s_w.shape[2]
    hm = b_i_m.shape[1]
    hs = dt_bias_s.shape[1]
    sizes = (dm, dm, hm, hm, dr, dr, ds, dconv, hs)
    offs = [0]
    for s in sizes:
        offs.append(offs[-1] + s)
    main_cols = [(offs[0], offs[2]), (offs[4], offs[8])]
    n_main = sum(b - a for a, b in main_cols)

    row = lambda v: v.reshape(1, -1).astype(F32)
    x2 = x.reshape(batch * seq, d)
    for l in range(depth):
        wl = w_in[l]
        gate_w = jnp.zeros((d, LANES), F32)
        gate_w = gate_w.at[:, GATE_I:GATE_I + hm].set(wl[:, offs[2]:offs[3]])
        gate_w = gate_w.at[:, GATE_F:GATE_F + hm].set(wl[:, offs[3]:offs[4]])
        gate_w = gate_w.at[:, GATE_DT:GATE_DT + hs].set(wl[:, offs[8]:offs[9]])
        w_all = jnp.concatenate([wl[:, a:b] for a, b in main_cols] + [gate_w], axis=1).astype(BF16)
        gate_b = jnp.zeros((1, LANES), F32)
        gate_b = gate_b.at[0, GATE_I:GATE_I + hm].set(b_i_m[l])
        gate_b = gate_b.at[0, GATE_F:GATE_F + hm].set(b_f_m[l])
        gate_b = gate_b.at[0, GATE_DT:GATE_DT + hs].set(dt_bias_s[l])
        alog = jnp.zeros((1, LANES), F32).at[0, GATE_DT:GATE_DT + hs].set(a_log_s[l])

        pm, pg = _in_proj(x2, row(norm_mix_pre[l]), w_all, n_main)
        y_m = _mlstm(pm, pg, conv_m_w[l], row(conv_m_b[l]), w_q_m[l].astype(BF16),
                     w_k_m[l].astype(BF16), w_v_m[l].astype(BF16), gate_b, row(norm_m[l]),
                     batch=batch, seq=seq)
        y_r = _rglru(pm, conv_r_w[l], row(conv_r_b[l]), _block_diag(w_a_r[l], 4).astype(BF16),
                     _block_diag(w_x_r[l], 4).astype(BF16), row(b_a_r[l]), row(b_x_r[l]),
                     row(lam_r[l]), batch=batch, seq=seq, col0=2 * dm // dr)
        y_s = _ssd(pm, pg, conv_s_w[l], row(conv_s_b[l]), gate_b, alog,
                   row(jnp.repeat(d_skip_s[l], SSD_HEAD_DIM)), row(norm_s[l]),
                   batch=batch, seq=seq, col0=(2 * dm + 2 * dr) // ds)
        x1, h2 = _out_proj(y_m, y_r, y_s, x2, w_out[l].astype(BF16), row(norm_mix_post[l]),
                           row(norm_ffn_pre[l]))
        x2 = _ffn(h2, x1, w_up[l].astype(BF16), conv_f_w[l], row(conv_f_b[l]),
                  w_down[l].astype(BF16), row(norm_ffn_post[l]), batch=batch, seq=seq)
    return x2.reshape(batch, seq, d)
```

```python
import functools
import math

import jax
import jax.numpy as jnp
from jax import lax
from jax.experimental import pallas as pl
from jax.experimental.pallas import tpu as pltpu

F32 = jnp.float32
BF16 = jnp.bfloat16
EPS = 1e-6

LANES = 128
HALO = 16
VMEM_LIMIT = 56 * 1024 * 1024

MLSTM_HEADS = 4
RGLRU_BLOCKS = 8
RGLRU_C = 8.0
SSD_HEAD_DIM = 64
SSD_GROUPS = 2
SSD_STATE = 128
GATE_I, GATE_F, GATE_DT = 0, 4, 8

ROWS_PROJ = 512
ROWS_FFN = 512
FFN_SUB = 256
CHUNK = 128
ROWS_LRU = 256
FFN_COLS = 256


def _rms(x, g):
    return x * lax.rsqrt(jnp.mean(x * x, axis=-1, keepdims=True) + EPS) * g


def _softplus(x):
    return jnp.maximum(x, 0.0) + jnp.log1p(jnp.exp(-jnp.abs(x)))


def _causal_conv(x, halo, w, b):
    k = w.shape[0]
    xe = jnp.concatenate([halo, x], axis=0)
    out = b + w[k - 1:k] * x
    for j in range(1, k):
        out = out + w[k - 1 - j:k - j] * pltpu.roll(xe, j, 0)[HALO:]
    return out


def _lane_pair(col_lo, col_hi, n, width):
    lane = lax.broadcasted_iota(jnp.int32, (n, width), 1)
    return jnp.where(lane < width // 2, col_lo, col_hi)


def _in_proj_kernel(x_ref, g_ref, w_ref, pm_ref, pg_ref, *, n_main, col_chunk):
    h = _rms(x_ref[...], g_ref[...]).astype(BF16)
    for c0 in range(0, n_main, col_chunk):
        pm_ref[:, c0:c0 + col_chunk] = jnp.dot(
            h, w_ref[:, c0:c0 + col_chunk], preferred_element_type=F32).astype(BF16)
    pg_ref[...] = jnp.dot(h, w_ref[:, n_main:], preferred_element_type=F32)


def _in_proj(x2, g, w, n_main):
    t, d = x2.shape
    n_all = w.shape[1]
    tm = ROWS_PROJ
    return pl.pallas_call(
        functools.partial(_in_proj_kernel, n_main=n_main, col_chunk=512),
        grid=(t // tm,),
        in_specs=[
            pl.BlockSpec((tm, d), lambda i: (i, 0)),
            pl.BlockSpec((1, d), lambda i: (0, 0)),
            pl.BlockSpec((d, n_all), lambda i: (0, 0)),
        ],
        out_specs=[
            pl.BlockSpec((tm, n_main), lambda i: (i, 0)),
            pl.BlockSpec((tm, n_all - n_main), lambda i: (i, 0)),
        ],
        out_shape=[
            jax.ShapeDtypeStruct((t, n_main), BF16),
            jax.ShapeDtypeStruct((t, n_all - n_main), F32),
        ],
        compiler_params=pltpu.CompilerParams(
            dimension_semantics=("arbitrary",), vmem_limit_bytes=VMEM_LIMIT),
        name="in_proj",
    )(x2, g, w)


def _mlstm_kernel(xm_ref, xh_ref, om_ref, gate_ref, cw_ref, cb_ref, wq_ref, wk_ref, wv_ref,
                  gb_ref, gn_ref, y_ref, c_ref, m_ref, *, n, heads, dh):
    t = pl.program_id(1)

    @pl.when(t == 0)
    def _init():
        c_ref[...] = jnp.zeros_like(c_ref)
        m_ref[...] = jnp.zeros_like(m_ref)

    xm = xm_ref[...].astype(F32)
    halo = jnp.where(t > 0, xh_ref[...].astype(F32), 0.0)
    xc = _causal_conv(xm, halo, cw_ref[...], cb_ref[...])
    xc = xc * jax.nn.sigmoid(xc)

    g = gate_ref[...] + gb_ref[...]
    lf = -_softplus(-g)
    row = lax.broadcasted_iota(jnp.int32, (n, n), 0)
    col = lax.broadcasted_iota(jnp.int32, (n, n), 1)
    causal = row >= col
    cs = jnp.dot(causal.astype(F32), lf, precision=lax.Precision.HIGHEST,
                 preferred_element_type=F32)
    cs_t = cs.T
    g_t = g.T
    ones = jnp.ones((n, dh), BF16)
    scale = dh ** -0.5

    for h in range(heads):
        sl = slice(h * dh, (h + 1) * dh)
        xc_h = xc[:, sl].astype(BF16)
        q = jnp.dot(xc_h, wq_ref[h], preferred_element_type=F32).astype(BF16)
        k = jnp.dot(xc_h, wk_ref[h], preferred_element_type=F32) * scale
        v = jnp.dot(xm_ref[:, sl], wv_ref[h], preferred_element_type=F32).astype(BF16)
        v_aug = jnp.concatenate([v, ones], axis=1)
        k_bf = k.astype(BF16)

        bcol = cs[:, GATE_F + h:GATE_F + h + 1]
        brow = cs_t[GATE_F + h:GATE_F + h + 1, :]
        icol = g[:, GATE_I + h:GATE_I + h + 1]
        irow = g_t[GATE_I + h:GATE_I + h + 1, :]
        btot = cs[n - 1:n, GATE_F + h:GATE_F + h + 1]
        m_prev = m_ref[h:h + 1, 0:1]

        d = jnp.where(causal, bcol - brow + irow, -jnp.inf)
        inter = bcol + m_prev
        m_t = jnp.maximum(inter, jnp.max(d, axis=1, keepdims=True))
        e_inter = jnp.exp(inter - m_t)
        scores = lax.dot_general(q, k_bf, (((1,), (1,)), ((), ())),
                                 preferred_element_type=F32) * jnp.exp(d - m_t)
        c_prev = c_ref[h]
        comb = (jnp.dot(scores.astype(BF16), v_aug, preferred_element_type=F32)
                + e_inter * jnp.dot(q, c_prev.astype(BF16), preferred_element_type=F32))
        num = comb[:, :dh]
        den = comb[:, dh:]
        hh = num / jnp.maximum(jnp.abs(den), jnp.exp(-m_t))

        wst = btot - bcol + icol
        m_loc = jnp.max(wst, axis=0, keepdims=True)
        ek = (jnp.exp(wst - m_loc) * k).astype(BF16)
        c_loc = lax.dot_general(ek, v_aug, (((0,), (0,)), ((), ())), preferred_element_type=F32)
        m_new = jnp.maximum(btot + m_prev, m_loc)
        c_ref[h] = jnp.exp(btot + m_prev - m_new) * c_prev + jnp.exp(m_loc - m_new) * c_loc
        m_ref[h:h + 1, :] = jnp.broadcast_to(m_new, (1, LANES))

        o = jax.nn.sigmoid(om_ref[:, sl].astype(F32)) * hh
        y_ref[:, sl] = _rms(o, gn_ref[:, sl]).astype(y_ref.dtype)


def _mlstm(pm, pg, cw, cb, wq, wk, wv, gb, gn, *, batch, seq):
    n = CHUNK
    heads = MLSTM_HEADS
    d = cw.shape[1]
    dh = d // heads
    nt = seq // n
    rows = lambda b, t: b * nt + t
    halo = lambda b, t: jnp.maximum((b * seq + t * n) // HALO - 1, 0)
    const2 = lambda b, t: (0, 0)
    const3 = lambda b, t: (0, 0, 0)
    return pl.pallas_call(
        functools.partial(_mlstm_kernel, n=n, heads=heads, dh=dh),
        grid=(batch, nt),
        in_specs=[
            pl.BlockSpec((n, d), lambda b, t: (rows(b, t), 0)),
            pl.BlockSpec((HALO, d), lambda b, t: (halo(b, t), 0)),
            pl.BlockSpec((n, d), lambda b, t: (rows(b, t), 1)),
            pl.BlockSpec((n, LANES), lambda b, t: (rows(b, t), 0)),
            pl.BlockSpec(cw.shape, const2),
            pl.BlockSpec(cb.shape, const2),
            pl.BlockSpec(wq.shape, const3),
            pl.BlockSpec(wk.shape, const3),
            pl.BlockSpec(wv.shape, const3),
            pl.BlockSpec(gb.shape, const2),
            pl.BlockSpec(gn.shape, const2),
        ],
        out_specs=pl.BlockSpec((n, d), lambda b, t: (rows(b, t), 0)),
        out_shape=jax.ShapeDtypeStruct((batch * seq, d), BF16),
        scratch_shapes=[pltpu.VMEM((heads, dh, 2 * dh), F32), pltpu.VMEM((8, LANES), F32)],
        compiler_params=pltpu.CompilerParams(
            dimension_semantics=("arbitrary", "arbitrary"), vmem_limit_bytes=VMEM_LIMIT),
        name="mlstm",
    )(pm, pm, pm, pg, cw, cb, wq, wk, wv, gb, gn)


def _rglru_kernel(xr_ref, xh_ref, yr_ref, cw_ref, cb_ref, wa_ref, wx_ref, ba_ref, bx_ref, lam_ref,
                  y_ref, h_ref, *, n, tile):
    t = pl.program_id(1)

    @pl.when(t == 0)
    def _init():
        h_ref[...] = jnp.zeros_like(h_ref)

    x = xr_ref[...].astype(F32)
    halo = jnp.where(t > 0, xh_ref[...].astype(F32), 0.0)
    xc = _causal_conv(x, halo, cw_ref[...], cb_ref[...])
    xc_bf = xc.astype(BF16)
    d = xc.shape[1]
    ra, ri = [], []
    for j in range(d // tile):
        blk = xc_bf[:, j * tile:(j + 1) * tile]
        ra.append(jnp.dot(blk, wa_ref[j], preferred_element_type=F32))
        ri.append(jnp.dot(blk, wx_ref[j], preferred_element_type=F32))
    r = jax.nn.sigmoid(jnp.concatenate(ra, axis=1) + ba_ref[...])
    i = jax.nn.sigmoid(jnp.concatenate(ri, axis=1) + bx_ref[...])
    log_a = (-RGLRU_C) * r * _softplus(-lam_ref[...])
    a = jnp.exp(log_a)
    u = jnp.sqrt(1.0 - a * a) * (i * xc)

    rmod = lax.broadcasted_iota(jnp.int32, (n, d), 0) & 7
    for s in (1, 2, 4):
        keep = rmod >= s
        u = jnp.where(keep, a * pltpu.roll(u, s, 0) + u, u)
        a = jnp.where(keep, a * pltpu.roll(a, s, 0), a)
    gel = jax.nn.gelu(yr_ref[...].astype(F32), approximate=True)
    carry = h_ref[...]
    for j in range(n // 8):
        rs = slice(j * 8, (j + 1) * 8)
        hj = a[rs] * carry + u[rs]
        y_ref[rs, :] = (hj * gel[rs]).astype(y_ref.dtype)
        carry = hj[7:8]
    h_ref[...] = carry


def _rglru(pm, cw, cb, wa, wx, ba, bx, lam, *, batch, seq, col0):
    n = ROWS_LRU
    d = cw.shape[1]
    nt = seq // n
    rows = lambda b, t: b * nt + t
    halo = lambda b, t: jnp.maximum((b * seq + t * n) // HALO - 1, 0)
    const2 = lambda b, t: (0, 0)
    const3 = lambda b, t: (0, 0, 0)
    return pl.pallas_call(
        functools.partial(_rglru_kernel, n=n, tile=wa.shape[1]),
        grid=(batch, nt),
        in_specs=[
            pl.BlockSpec((n, d), lambda b, t: (rows(b, t), col0)),
            pl.BlockSpec((HALO, d), lambda b, t: (halo(b, t), col0)),
            pl.BlockSpec((n, d), lambda b, t: (rows(b, t), col0 + 1)),
            pl.BlockSpec(cw.shape, const2),
            pl.BlockSpec(cb.shape, const2),
            pl.BlockSpec(wa.shape, const3),
            pl.BlockSpec(wx.shape, const3),
            pl.BlockSpec(ba.shape, const2),
            pl.BlockSpec(bx.shape, const2),
            pl.BlockSpec(lam.shape, const2),
        ],
        out_specs=pl.BlockSpec((n, d), lambda b, t: (rows(b, t), 0)),
        out_shape=jax.ShapeDtypeStruct((batch * seq, d), BF16),
        scratch_shapes=[pltpu.VMEM((1, d), F32)],
        compiler_params=pltpu.CompilerParams(
            dimension_semantics=("arbitrary", "arbitrary"), vmem_limit_bytes=VMEM_LIMIT),
        name="rglru",
    )(pm, pm, pm, cw, cb, wa, wx, ba, bx, lam)


def _ssd_kernel(z_ref, xs_ref, xsh_ref, bc_ref, bch_ref, gate_ref, cw_ref, cb_ref, gb_ref, alog_ref,
                dskip_ref, gn_ref, y_ref, s_ref, *, n, heads, hd, groups, ns):
    t = pl.program_id(1)

    @pl.when(t == 0)
    def _init():
        s_ref[...] = jnp.zeros_like(s_ref)

    d = heads * hd
    cw = cw_ref[...]
    cb = cb_ref[...]
    live = t > 0
    xs = _causal_conv(xs_ref[...].astype(F32), jnp.where(live, xsh_ref[...].astype(F32), 0.0),
                      cw[:, :d], cb[:, :d])
    xs = xs * jax.nn.sigmoid(xs)
    bc = _causal_conv(bc_ref[...].astype(F32), jnp.where(live, bch_ref[...].astype(F32), 0.0),
                      cw[:, d:], cb[:, d:])
    bc = bc * jax.nn.sigmoid(bc)

    dt = _softplus(gate_ref[...] + gb_ref[...])
    a = dt * (-jnp.exp(alog_ref[...]))
    row = lax.broadcasted_iota(jnp.int32, (n, n), 0)
    col = lax.broadcasted_iota(jnp.int32, (n, n), 1)
    causal = row >= col
    acs = jnp.dot(causal.astype(F32), a, precision=lax.Precision.HIGHEST,
                  preferred_element_type=F32)
    acs_t = acs.T

    pair = 2 * hd
    hpg = heads // groups
    lane = lax.broadcasted_iota(jnp.int32, (n, pair), 1)
    lo = lane < hd
    ys = []
    for grp in range(groups):
        b_g = bc[:, grp * ns:(grp + 1) * ns].astype(BF16)
        c_g = bc[:, (groups + grp) * ns:(groups + grp + 1) * ns].astype(BF16)
        cbm = lax.dot_general(c_g, b_g, (((1,), (1,)), ((), ())), preferred_element_type=F32)
        for pp in range(hpg // 2):
            h0 = grp * hpg + 2 * pp
            ps = slice(h0 * hd, h0 * hd + pair)
            cols = [GATE_DT + h0, GATE_DT + h0 + 1]
            acol = [acs[:, c:c + 1] for c in cols]
            arow = [acs_t[c:c + 1, :] for c in cols]
            atot = [acs[n - 1:n, c:c + 1] for c in cols]
            dt_p = _lane_pair(dt[:, cols[0]:cols[0] + 1], dt[:, cols[1]:cols[1] + 1], n, pair)
            xs_p = xs[:, ps]
            xdt = xs_p * dt_p
            ydiag = jnp.zeros((n, pair), F32)
            for e in range(2):
                dec = jnp.exp(jnp.where(causal, acol[e] - arow[e], -jnp.inf))
                half = jnp.where(lo if e == 0 else jnp.logical_not(lo), xdt, 0.0).astype(BF16)
                ydiag = ydiag + jnp.dot((cbm * dec).astype(BF16), half, preferred_element_type=F32)
            s_prev = s_ref[:, ps]
            yoff = (jnp.dot(c_g, s_prev.astype(BF16), preferred_element_type=F32)
                    * jnp.exp(_lane_pair(acol[0], acol[1], n, pair)))
            dst = jnp.exp(_lane_pair(atot[0] - acol[0], atot[1] - acol[1], n, pair))
            s_loc = lax.dot_general(b_g, (xdt * dst).astype(BF16), (((0,), (0,)), ((), ())),
                                    preferred_element_type=F32)
            s_ref[:, ps] = jnp.exp(_lane_pair(atot[0], atot[1], 1, pair)) * s_prev + s_loc
            ys.append(ydiag + yoff + xs_p * dskip_ref[:, ps])
    y = jnp.concatenate(ys, axis=1)
    z = z_ref[...].astype(F32)
    y_ref[...] = _rms(y * (z * jax.nn.sigmoid(z)), gn_ref[...]).astype(y_ref.dtype)


def _ssd(pm, pg, cw, cb, gb, alog, dskip, gn, *, batch, seq, col0):
    n = CHUNK
    d = gn.shape[1]
    heads = d // SSD_HEAD_DIM
    nt = seq // n
    rows = lambda b, t: b * nt + t
    halo = lambda b, t: jnp.maximum((b * seq + t * n) // HALO - 1, 0)
    const2 = lambda b, t: (0, 0)
    return pl.pallas_call(
        functools.partial(_ssd_kernel, n=n, heads=heads, hd=SSD_HEAD_DIM, groups=SSD_GROUPS,
                          ns=SSD_STATE),
        grid=(batch, nt),
        in_specs=[
            pl.BlockSpec((n, d), lambda b, t: (rows(b, t), col0)),
            pl.BlockSpec((n, d), lambda b, t: (rows(b, t), col0 + 1)),
            pl.BlockSpec((HALO, d), lambda b, t: (halo(b, t), col0 + 1)),
            pl.BlockSpec((n, d), lambda b, t: (rows(b, t), col0 + 2)),
            pl.BlockSpec((HALO, d), lambda b, t: (halo(b, t), col0 + 2)),
            pl.BlockSpec((n, LANES), lambda b, t: (rows(b, t), 0)),
            pl.BlockSpec(cw.shape, const2),
            pl.BlockSpec(cb.shape, const2),
            pl.BlockSpec(gb.shape, const2),
            pl.BlockSpec(alog.shape, const2),
            pl.BlockSpec(dskip.shape, const2),
            pl.BlockSpec(gn.shape, const2),
        ],
        out_specs=pl.BlockSpec((n, d), lambda b, t: (rows(b, t), 0)),
        out_shape=jax.ShapeDtypeStruct((batch * seq, d), BF16),
        scratch_shapes=[pltpu.VMEM((SSD_STATE, d), F32)],
        compiler_params=pltpu.CompilerParams(
            dimension_semantics=("arbitrary", "arbitrary"), vmem_limit_bytes=VMEM_LIMIT),
        name="ssd",
    )(pm, pm, pm, pm, pm, pg, cw, cb, gb, alog, dskip, gn)


def _out_proj_kernel(ym_ref, yr_ref, ys_ref, x_ref, wo_ref, gpost_ref, gpre_ref, x1_ref, h2_ref, *, d_grp):
    mix = (jnp.dot(ym_ref[...], wo_ref[0:d_grp, :], preferred_element_type=F32)
           + jnp.dot(yr_ref[...], wo_ref[d_grp:2 * d_grp, :], preferred_element_type=F32)
           + jnp.dot(ys_ref[...], wo_ref[2 * d_grp:3 * d_grp, :], preferred_element_type=F32))
    x1 = x_ref[...] + _rms(mix, gpost_ref[...])
    x1_ref[...] = x1
    h2_ref[...] = _rms(x1, gpre_ref[...]).astype(h2_ref.dtype)


def _out_proj(ym, yr, ys, x2, wo, gpost, gpre):
    t, d = x2.shape
    d_grp = ym.shape[1]
    tm = ROWS_PROJ
    rowblk = lambda i: (i, 0)
    const = lambda i: (0, 0)
    return pl.pallas_call(
        functools.partial(_out_proj_kernel, d_grp=d_grp),
        grid=(t // tm,),
        in_specs=[
            pl.BlockSpec((tm, d_grp), rowblk),
            pl.BlockSpec((tm, d_grp), rowblk),
            pl.BlockSpec((tm, d_grp), rowblk),
            pl.BlockSpec((tm, d), rowblk),
            pl.BlockSpec(wo.shape, const),
            pl.BlockSpec((1, d), const),
            pl.BlockSpec((1, d), const),
        ],
        out_specs=[pl.BlockSpec((tm, d), rowblk), pl.BlockSpec((tm, d), rowblk)],
        out_shape=[jax.ShapeDtypeStruct((t, d), F32), jax.ShapeDtypeStruct((t, d), BF16)],
        compiler_params=pltpu.CompilerParams(
            dimension_semantics=("arbitrary",), vmem_limit_bytes=VMEM_LIMIT),
        name="out_proj",
    )(ym, yr, ys, x2, wo, gpost, gpre)


def _ffn_kernel(h_ref, hh_ref, x_ref, wup_ref, cw_ref, cb_ref, wdn_ref, gpost_ref, o_ref, he_ref, acc_ref,
                *, d_ff, cols, sub):
    t = pl.program_id(1)
    tm = h_ref.shape[0]
    he_ref[0:HALO, :] = jnp.where(t > 0, hh_ref[...], jnp.zeros_like(hh_ref))
    he_ref[HALO:, :] = h_ref[...]

    def up(item):
        r0, c0 = item
        he = he_ref[r0:r0 + HALO + sub, :]
        return [jnp.dot(he, wup_ref[:, base:base + cols], preferred_element_type=F32)
                for base in (c0, d_ff + c0)]

    def gate(item, us):
        _, c0 = item
        branch = []
        for base, u in zip((c0, d_ff + c0), us):
            cs = slice(base, base + cols)
            w = cw_ref[:, cs]
            branch.append(cb_ref[:, cs] + w[2:3] * u[HALO:] + w[1:2] * pltpu.roll(u, 1, 0)[HALO:]
                          + w[0:1] * pltpu.roll(u, 2, 0)[HALO:])
        return (jax.nn.gelu(branch[0], approximate=True) * branch[1]).astype(BF16)

    items = [(r0, c0) for r0 in range(0, tm, sub) for c0 in range(0, d_ff, cols)]
    us = up(items[0])
    for i, item in enumerate(items):
        us_next = up(items[i + 1]) if i + 1 < len(items) else None
        r0, c0 = item
        rs = slice(r0, r0 + sub)
        down = jnp.dot(gate(item, us), wdn_ref[c0:c0 + cols, :], preferred_element_type=F32)
        if c0 == 0:
            acc_ref[rs, :] = down
        else:
            acc_ref[rs, :] += down
        if c0 + cols >= d_ff:
            o_ref[rs, :] = x_ref[rs, :] + _rms(acc_ref[rs, :], gpost_ref[...])
        us = us_next


def _ffn(h2, x1, wup, cw, cb, wdn, gpost, *, batch, seq):
    t, d = x1.shape
    d_ff = wdn.shape[0]
    tm = ROWS_FFN
    nt = seq // tm
    rows = lambda b, i: (b * nt + i, 0)
    halo = lambda b, i: (jnp.maximum((b * seq + i * tm) // HALO - 1, 0), 0)
    const = lambda b, i: (0, 0)
    return pl.pallas_call(
        functools.partial(_ffn_kernel, d_ff=d_ff, cols=FFN_COLS, sub=FFN_SUB),
        grid=(batch, nt),
        in_specs=[
            pl.BlockSpec((tm, d), rows),
            pl.BlockSpec((HALO, d), halo),
            pl.BlockSpec((tm, d), rows),
            pl.BlockSpec(wup.shape, const),
            pl.BlockSpec(cw.shape, const),
            pl.BlockSpec(cb.shape, const),
            pl.BlockSpec(wdn.shape, const),
            pl.BlockSpec((1, d), const),
        ],
        out_specs=pl.BlockSpec((tm, d), rows),
        out_shape=jax.ShapeDtypeStruct((t, d), F32),
        scratch_shapes=[pltpu.VMEM((HALO + tm, d), BF16), pltpu.VMEM((tm, d), F32)],
        compiler_params=pltpu.CompilerParams(
            dimension_semantics=("arbitrary", "arbitrary"), vmem_limit_bytes=VMEM_LIMIT),
        name="ffn",
    )(h2, h2, x1, wup, cw, cb, wdn, gpost)


def _block_diag(w, per_tile):
    nb, e, _ = w.shape
    w = w.reshape(nb // per_tile, per_tile, e, e)
    eye = jnp.eye(per_tile, dtype=w.dtype)
    out = jnp.einsum("tpij,pq->tpiqj", w, eye)
    return out.reshape(nb // per_tile, per_tile * e, per_tile * e)


def kernel(x, norm_mix_pre, norm_mix_post, norm_ffn_pre, norm_ffn_post, w_in, conv_m_w, conv_m_b, w_q_m, w_k_m, w_v_m, b_i_m, b_f_m, norm_m, conv_r_w, conv_r_b, w_a_r, b_a_r, w_x_r, b_x_r, lam_r, conv_s_w, conv_s_b, dt_bias_s, a_log_s, d_skip_s, norm_s, w_out, w_up, conv_f_w, conv_f_b, w_down):
    batch, seq, d = x.shape
    depth = w_in.shape[0]
    dm = conv_m_w.shape[2]
    dr = conv_r_w.shape[2]
    ds = norm_s.shape[1]
    dconv = conv_s_w.shape[2]
    hm = b_i_m.shape[1]
    hs = dt_bias_s.shape[1]
    sizes = (dm, dm, hm, hm, dr, dr, ds, dconv, hs)
    offs = [0]
    for s in sizes:
        offs.append(offs[-1] + s)
    main_cols = [(offs[0], offs[2]), (offs[4], offs[8])]
    n_main = sum(b - a for a, b in main_cols)

    row = lambda v: v.reshape(1, -1).astype(F32)
    x2 = x.reshape(batch * seq, d)
    for l in range(depth):
        wl = w_in[l]
        gate_w = jnp.zeros((d, LANES), F32)
        gate_w = gate_w.at[:, GATE_I:GATE_I + hm].set(wl[:, offs[2]:offs[3]])
        gate_w = gate_w.at[:, GATE_F:GATE_F + hm].set(wl[:, offs[3]:offs[4]])
        gate_w = gate_w.at[:, GATE_DT:GATE_DT + hs].set(wl[:, offs[8]:offs[9]])
        w_all = jnp.concatenate([wl[:, a:b] for a, b in main_cols] + [gate_w], axis=1).astype(BF16)
        gate_b = jnp.zeros((1, LANES), F32)
        gate_b = gate_b.at[0, GATE_I:GATE_I + hm].set(b_i_m[l])
        gate_b = gate_b.at[0, GATE_F:GATE_F + hm].set(b_f_m[l])
        gate_b = gate_b.at[0, GATE_DT:GATE_DT + hs].set(dt_bias_s[l])
        alog = jnp.zeros((1, LANES), F32).at[0, GATE_DT:GATE_DT + hs].set(a_log_s[l])

        pm, pg = _in_proj(x2, row(norm_mix_pre[l]), w_all, n_main)
        y_m = _mlstm(pm, pg, conv_m_w[l], row(conv_m_b[l]), w_q_m[l].astype(BF16),
                     w_k_m[l].astype(BF16), w_v_m[l].astype(BF16), gate_b, row(norm_m[l]),
                     batch=batch, seq=seq)
        y_r = _rglru(pm, conv_r_w[l], row(conv_r_b[l]), _block_diag(w_a_r[l], 4).astype(BF16),
                     _block_diag(w_x_r[l], 4).astype(BF16), row(b_a_r[l]), row(b_x_r[l]),
                     row(lam_r[l]), batch=batch, seq=seq, col0=2 * dm // dr)
        y_s = _ssd(pm, pg, conv_s_w[l], row(conv_s_b[l]), gate_b, alog,
                   row(jnp.repeat(d_skip_s[l], SSD_HEAD_DIM)), row(norm_s[l]),
                   batch=batch, seq=seq, col0=(2 * dm + 2 * dr) // ds)
        x1, h2 = _out_proj(y_m, y_r, y_s, x2, w_out[l].astype(BF16), row(norm_mix_post[l]),
                           row(norm_ffn_pre[l]))
        x2 = _ffn(h2, x1, w_up[l].astype(BF16), conv_f_w[l], row(conv_f_b[l]),
                  w_down[l].astype(BF16), row(norm_ffn_post[l]), batch=batch, seq=seq)
    return x2.reshape(batch, seq, d)
```

```python
import functools

import jax
import jax.numpy as jnp
from jax import lax
from jax.experimental import pallas as pl
from jax.experimental.pallas import tpu as pltpu

F32 = jnp.float32
BF16 = jnp.bfloat16
EPS = 1e-6

LANES = 128
SUBLANES = 8
VMEM_LIMIT = 56 * 1024 * 1024

MLSTM_HEADS = 4
RGLRU_C = 8.0
SSD_HEAD_DIM = 64
SSD_GROUPS = 2
SSD_STATE = 128
GATE_I, GATE_F, GATE_DT = 0, 4, 8

PERM = 128
ROWS_FRONT = 256
FRONT_COLS = 256
ROWS_FFN = 512
FFN_SUB = 256
FFN_COLS = 256
FFN_HALO = 16


def _rms(x, g):
    return x * lax.rsqrt(jnp.mean(x * x, axis=-1, keepdims=True) + EPS) * g


def _softplus(x):
    return jnp.maximum(x, 0.0) + jnp.log(1.0 + jnp.exp(-jnp.abs(x)))


def _sigmoid(x):
    return 0.5 * jnp.tanh(0.5 * x) + 0.5


def _silu(x):
    hx = 0.5 * x
    return hx * jnp.tanh(hx) + hx


def _wrap_tail(prev_tail, tail):
    out = []
    for j in range(tail.shape[0] // SUBLANES):
        rs = slice(j * SUBLANES, (j + 1) * SUBLANES)
        sub = lax.broadcasted_iota(jnp.int32, (SUBLANES, tail.shape[1]), 0)
        mixed = jnp.where(sub == SUBLANES - 1, prev_tail[rs], tail[rs])
        out.append(pltpu.roll(mixed, 1, 0))
    return jnp.concatenate(out, axis=0)


def _perm_conv(x, prev_tail, w, b):
    n = x.shape[0]
    k = w.shape[0]
    nt = SUBLANES * (k - 1)
    ext = jnp.concatenate([_wrap_tail(prev_tail, x[n - nt:]), x], axis=0)
    out = b + w[k - 1:k] * x
    for j in range(1, k):
        start = nt - SUBLANES * j
        out = out + w[k - 1 - j:k - j] * ext[start:start + n]
    return out


def _causal_mask(n):
    seg = n // SUBLANES
    row = lax.broadcasted_iota(jnp.int32, (n, n), 0)
    col = lax.broadcasted_iota(jnp.int32, (n, n), 1)
    t_row = (row & (SUBLANES - 1)) * seg + (row >> 3)
    t_col = (col & (SUBLANES - 1)) * seg + (col >> 3)
    return t_row >= t_col


def _lane_pair(col_lo, col_hi, n, width):
    lane = lax.broadcasted_iota(jnp.int32, (n, width), 1)
    return jnp.where(lane < width // 2, col_lo, col_hi)


def _mlstm_project(xc, xm_bf, wq_ref, wk_ref, wv_ref, h, *, n, dh):
    sl = slice(h * dh, (h + 1) * dh)
    xc_h = xc[:, sl].astype(BF16)
    q = jnp.dot(xc_h, wq_ref[h], preferred_element_type=F32).astype(BF16)
    k = jnp.dot(xc_h, wk_ref[h], preferred_element_type=F32) * (dh ** -0.5)
    v = jnp.dot(xm_bf[:, sl], wv_ref[h], preferred_element_type=F32).astype(BF16)
    return q, k, jnp.concatenate([v, jnp.ones((n, dh), BF16)], axis=1)


def _mlstm_decay(g, cs, cs_t, g_t, causal, m_prev, h, *, n):
    bcol = cs[:, GATE_F + h:GATE_F + h + 1]
    brow = cs_t[GATE_F + h:GATE_F + h + 1, :]
    icol = g[:, GATE_I + h:GATE_I + h + 1]
    irow = g_t[GATE_I + h:GATE_I + h + 1, :]
    btot = cs[n - 1:n, GATE_F + h:GATE_F + h + 1]
    d = jnp.where(causal, bcol - brow + irow, -jnp.inf)
    inter = bcol + m_prev
    m_t = jnp.maximum(inter, jnp.max(d, axis=1, keepdims=True))
    wst = btot - bcol + icol
    m_loc = jnp.max(wst, axis=0, keepdims=True)
    m_new = jnp.maximum(btot + m_prev, m_loc)
    return dict(pmat=jnp.exp(d - m_t), e_inter=jnp.exp(inter - m_t), floor=jnp.exp(-m_t),
                ew=jnp.exp(wst - m_loc), s_prev=jnp.exp(btot + m_prev - m_new),
                s_loc=jnp.exp(m_loc - m_new), m_new=m_new)


def _mlstm_output(q, k, v_aug, dec, c_prev, *, dh):
    scores = lax.dot_general(q, k.astype(BF16), (((1,), (1,)), ((), ())),
                             preferred_element_type=F32) * dec["pmat"]
    comb = (jnp.dot(scores.astype(BF16), v_aug, preferred_element_type=F32)
            + dec["e_inter"] * jnp.dot(q, c_prev.astype(BF16), preferred_element_type=F32))
    hh = comb[:, :dh] / jnp.maximum(jnp.abs(comb[:, dh:]), dec["floor"])
    ek = (dec["ew"] * k).astype(BF16)
    c_loc = lax.dot_general(ek, v_aug, (((0,), (0,)), ((), ())), preferred_element_type=F32)
    return hh, dec["s_prev"] * c_prev + dec["s_loc"] * c_loc


def _rglru_scan(a, u, carry):
    n, d = a.shape
    tiles = n // SUBLANES
    hs, ps = [u[0:SUBLANES]], [a[0:SUBLANES]]
    for i in range(1, tiles):
        rs = slice(i * SUBLANES, (i + 1) * SUBLANES)
        hs.append(a[rs] * hs[-1] + u[rs])
        ps.append(a[rs] * ps[-1])
    gacc, pacc = hs[-1], ps[-1]
    sub = lax.broadcasted_iota(jnp.int32, (SUBLANES, d), 0)
    for s in (1, 2, 4):
        keep = sub >= s
        gacc = jnp.where(keep, pacc * pltpu.roll(gacc, s, 0) + gacc, gacc)
        pacc = jnp.where(keep, pacc * pltpu.roll(pacc, s, 0), pacc)
    seg_end = gacc + pacc * carry
    seg_in = jnp.where(sub == 0, carry, pltpu.roll(seg_end, 1, 0))
    out = [hs[i] + ps[i] * seg_in for i in range(tiles)]
    return jnp.concatenate(out, axis=0), seg_end[SUBLANES - 1:SUBLANES]


def _ssd_decay(dt, acs, acs_t, causal, h0, *, n, hd):
    pair = 2 * hd
    cols = [GATE_DT + h0, GATE_DT + h0 + 1]
    acol = [acs[:, c:c + 1] for c in cols]
    arow = [acs_t[c:c + 1, :] for c in cols]
    atot = [acs[n - 1:n, c:c + 1] for c in cols]
    return dict(
        dec=[jnp.exp(jnp.where(causal, acol[e] - arow[e], -jnp.inf)) for e in range(2)],
        dt=_lane_pair(dt[:, cols[0]:cols[0] + 1], dt[:, cols[1]:cols[1] + 1], n, pair),
        e_in=jnp.exp(_lane_pair(acol[0], acol[1], n, pair)),
        e_out=jnp.exp(_lane_pair(atot[0] - acol[0], atot[1] - acol[1], n, pair)),
        e_tot=jnp.exp(_lane_pair(atot[0], atot[1], 1, pair)))


def _ssd_output(xs_p, b_g, c_g, cbm, dec, s_prev, dskip_p, *, n, hd):
    pair = 2 * hd
    lo = lax.broadcasted_iota(jnp.int32, (n, pair), 1) < hd
    xdt = xs_p * dec["dt"]
    ydiag = jnp.zeros((n, pair), F32)
    for e in range(2):
        half = jnp.where(lo if e == 0 else jnp.logical_not(lo), xdt, 0.0).astype(BF16)
        ydiag = ydiag + jnp.dot((cbm * dec["dec"][e]).astype(BF16), half, preferred_element_type=F32)
    yoff = jnp.dot(c_g, s_prev.astype(BF16), preferred_element_type=F32) * dec["e_in"]
    s_loc = lax.dot_general(b_g, (xdt * dec["e_out"]).astype(BF16), (((0,), (0,)), ((), ())),
                            preferred_element_type=F32)
    return ydiag + yoff + xs_p * dskip_p, dec["e_tot"] * s_prev + s_loc


def _front_kernel(xn_ref, xres_ref, gin_ref, win_ref, cw_ref, cb_ref, wq_ref, wk_ref, wv_ref, gb_ref, gnm_ref,
                  wa_ref, wx_ref, ba_ref, bx_ref, lam_ref, alog_ref, dskip_ref, gns_ref,
                  wo_ref, gpost_ref, gffn_ref, x1_ref, h2_ref,
                  pbuf, ybuf, tail_ref, c_ref, m_ref, h_ref, s_ref, *, n, steps_per_seq, dm, dr, ds, n_main):
    s = pl.program_id(0)

    @pl.when(s == 0)
    def _first():
        pbuf[1] = jnp.zeros(pbuf.shape[1:], pbuf.dtype)
        ybuf[...] = jnp.zeros_like(ybuf)

    @pl.when(lax.rem(jnp.maximum(s - 1, 0), steps_per_seq) == 0)
    def _sequence_start():
        tail_ref[...] = jnp.zeros_like(tail_ref)
        c_ref[...] = jnp.zeros_like(c_ref)
        m_ref[...] = jnp.zeros_like(m_ref)
        h_ref[...] = jnp.zeros_like(h_ref)
        s_ref[...] = jnp.zeros_like(s_ref)

    slot_w = lax.rem(s, 2)
    p_out = pbuf.at[slot_w]
    p_ref = pbuf.at[1 - slot_w]

    heads_m = MLSTM_HEADS
    dh = dm // heads_m
    hd = SSD_HEAD_DIM
    pair = 2 * hd
    n_pairs = ds // pair
    pairs_per_group = n_pairs // SSD_GROUPS
    n_all = win_ref.shape[1]
    n_conv = dm + dr + 2 * ds
    o0 = n_conv
    ns = SSD_STATE

    mix = jnp.dot(ybuf[...], wo_ref[...], preferred_element_type=F32)

    h_next = _rms(xn_ref[...], gin_ref[...]).astype(BF16)
    starts = list(range(0, n_all, FRONT_COLS))
    chunks = n // PERM
    per_call = -(-len(starts) // (8 * chunks))

    def project(k=per_call):
        for _ in range(k):
            if starts:
                c0 = starts.pop(0)
                c1 = min(c0 + FRONT_COLS, n_all)
                p_out[:, c0:c1] = jnp.dot(h_next, win_ref[:, c0:c1], preferred_element_type=F32)

    causal = _causal_mask(PERM)
    tri = causal.astype(F32)
    lane = lax.broadcasted_iota(jnp.int32, (1, LANES), 1)

    for ch in range(chunks):
        rows = slice(ch * PERM, (ch + 1) * PERM)

        g = p_ref[rows, n_main:] + gb_ref[...]
        sp = _softplus(jnp.where(lane < GATE_DT, -g, g))
        dt = sp
        steps = jnp.where(lane < GATE_DT, -sp, sp * (-jnp.exp(alog_ref[...])))
        cs = jnp.dot(tri, steps, precision=lax.Precision.HIGHEST, preferred_element_type=F32)
        project()

        nt_rows = tail_ref.shape[0]
        x_in = p_ref[rows, :n_conv]
        conv = _perm_conv(x_in, tail_ref[...], cw_ref[...], cb_ref[...])
        tail_ref[...] = x_in[PERM - nt_rows:]
        project()
        xc_m = _silu(conv[:, :dm])
        xc_r = conv[:, dm:dm + dr]
        xs_bc = _silu(conv[:, dm + dr:])
        cs_t, g_t = cs.T, g.T

        if ch == 0:
            x1 = xres_ref[...] + _rms(mix, gpost_ref[...])
            x1_ref[...] = x1
            h2_ref[...] = _rms(x1, gffn_ref[...]).astype(h2_ref.dtype)
        project()

        xm_bf = x_in[:, :dm].astype(BF16)
        qkv = [_mlstm_project(xc_m, xm_bf, wq_ref, wk_ref, wv_ref, h, n=PERM, dh=dh) for h in range(heads_m)]
        xr_bf = xc_r.astype(BF16)
        tile = wa_ref.shape[1]
        ra = [jnp.dot(xr_bf[:, j * tile:(j + 1) * tile], wa_ref[j], preferred_element_type=F32)
              for j in range(dr // tile)]
        ri = [jnp.dot(xr_bf[:, j * tile:(j + 1) * tile], wx_ref[j], preferred_element_type=F32)
              for j in range(dr // tile)]
        b_g = [xs_bc[:, ds + grp * ns:ds + (grp + 1) * ns].astype(BF16) for grp in range(SSD_GROUPS)]
        c_g = [xs_bc[:, ds + (SSD_GROUPS + grp) * ns:ds + (SSD_GROUPS + grp + 1) * ns].astype(BF16)
               for grp in range(SSD_GROUPS)]
        cbm = [lax.dot_general(c_g[grp], b_g[grp], (((1,), (1,)), ((), ())), preferred_element_type=F32)
               for grp in range(SSD_GROUPS)]

        r = _sigmoid(jnp.concatenate(ra, axis=1) + ba_ref[...])
        i = _sigmoid(jnp.concatenate(ri, axis=1) + bx_ref[...])
        a = jnp.exp((-RGLRU_C) * r * _softplus(-lam_ref[...]))
        w = 1.0 - a * a
        u = jnp.where(w > 0.0, w * lax.rsqrt(w), 0.0) * (i * xc_r)
        project()
        hseq, h_ref[...] = _rglru_scan(a, u, h_ref[...])
        yr = p_ref[rows, o0 + dm:o0 + dm + dr]
        ybuf[rows, dm:dm + dr] = (hseq * jax.nn.gelu(yr, approximate=True)).astype(ybuf.dtype)
        project()

        dec_m = [_mlstm_decay(g, cs, cs_t, g_t, causal, m_ref[h:h + 1, 0:1], h, n=PERM) for h in range(heads_m)]
        dec_s = [_ssd_decay(dt, cs, cs_t, causal, 2 * p, n=PERM, hd=hd) for p in range(n_pairs)]
        ys = []
        for j in range(max(heads_m, n_pairs)):
            if j < heads_m:
                sl = slice(j * dh, (j + 1) * dh)
                hh, c_ref[j] = _mlstm_output(*qkv[j], dec_m[j], c_ref[j], dh=dh)
                m_ref[j:j + 1, :] = jnp.broadcast_to(dec_m[j]["m_new"], (1, LANES))
                o = _sigmoid(p_ref[rows, o0 + j * dh:o0 + (j + 1) * dh]) * hh
                ybuf[rows, sl] = _rms(o, gnm_ref[:, sl]).astype(ybuf.dtype)
            if j % 2 == 1:
                project()
            if j < n_pairs:
                ps = slice(j * pair, (j + 1) * pair)
                grp = j // pairs_per_group
                y_p, s_ref[:, ps] = _ssd_output(xs_bc[:, ps], b_g[grp], c_g[grp], cbm[grp], dec_s[j],
                                                s_ref[:, ps], dskip_ref[:, ps], n=PERM, hd=hd)
                ys.append(y_p)
        project()
        z = p_ref[rows, o0 + dm + dr:o0 + dm + dr + ds]
        ybuf[rows, dm + dr:] = _rms(jnp.concatenate(ys, axis=1) * _silu(z), gns_ref[...]).astype(ybuf.dtype)
    project(len(starts))


def _front(x2, gin, win, cw, cb, wq, wk, wv, gb, gnm, wa, wx, ba, bx, lam, alog, dskip, gns, wo, gpost, gffn,
           *, batch, seq, n_main):
    n = ROWS_FRONT
    t, d = x2.shape
    dm, dr, ds = gnm.shape[1], lam.shape[1], gns.shape[1]
    nt = seq // n
    nblk = batch * nt
    k = cw.shape[0]
    consts = [gin, win, cw, cb, wq, wk, wv, gb, gnm, wa, wx, ba, bx, lam, alog, dskip, gns, wo, gpost, gffn]
    const_spec = lambda a: pl.BlockSpec(a.shape, lambda s, _nd=a.ndim: (0,) * _nd)
    nxt = lambda s: (jnp.minimum(s, nblk - 1), 0)
    done = lambda s: (jnp.clip(s - 2, 0, nblk - 1), 0)
    return pl.pallas_call(
        functools.partial(_front_kernel, n=n, steps_per_seq=nt, dm=dm, dr=dr, ds=ds, n_main=n_main),
        grid=(nblk + 2,),
        in_specs=[pl.BlockSpec((n, d), nxt), pl.BlockSpec((n, d), done)] + [const_spec(a) for a in consts],
        out_specs=[pl.BlockSpec((n, d), done), pl.BlockSpec((n, d), done)],
        out_shape=[jax.ShapeDtypeStruct((t, d), F32), jax.ShapeDtypeStruct((t, d), BF16)],
        scratch_shapes=[
            pltpu.VMEM((2, n, win.shape[1]), F32),
            pltpu.VMEM((n, dm + dr + ds), BF16),
            pltpu.VMEM((SUBLANES * (k - 1), cw.shape[1]), F32),
            pltpu.VMEM((MLSTM_HEADS, dm // MLSTM_HEADS, 2 * dm // MLSTM_HEADS), F32),
            pltpu.VMEM((SUBLANES, LANES), F32),
            pltpu.VMEM((1, dr), F32),
            pltpu.VMEM((SSD_STATE, ds), F32),
        ],
        compiler_params=pltpu.CompilerParams(
            dimension_semantics=("arbitrary",), vmem_limit_bytes=VMEM_LIMIT),
        name="front",
    )(x2, x2, *consts)


def _ffn_kernel(h_ref, hh_ref, x_ref, wup_ref, cw_ref, cb_ref, wdn_ref, gpost_ref, o_ref, he_ref, acc_ref,
                *, d_ff, cols, sub):
    t = pl.program_id(1)
    tm = h_ref.shape[0]
    halo = FFN_HALO
    he_ref[0:halo, :] = jnp.where(t > 0, hh_ref[...], jnp.zeros_like(hh_ref))
    he_ref[halo:, :] = h_ref[...]

    def up(item):
        r0, c0 = item
        he = he_ref[r0:r0 + halo + sub, :]
        return [jnp.dot(he, wup_ref[:, base:base + cols], preferred_element_type=F32)
                for base in (c0, d_ff + c0)]

    def gate(item, us):
        _, c0 = item
        branch = []
        for base, u in zip((c0, d_ff + c0), us):
            cs = slice(base, base + cols)
            w = cw_ref[:, cs]
            b = cb_ref[:, cs]
            parts = []
            for p0 in range(0, sub, PERM):
                blk = u[halo + p0:halo + p0 + PERM]
                parts.append(_perm_conv(blk, u[p0:p0 + halo], w, b))
            branch.append(jnp.concatenate(parts, axis=0))
        return (jax.nn.gelu(branch[0], approximate=True) * branch[1]).astype(BF16)

    items = [(r0, c0) for r0 in range(0, tm, sub) for c0 in range(0, d_ff, cols)]
    us = up(items[0])
    for i, item in enumerate(items):
        us_next = up(items[i + 1]) if i + 1 < len(items) else None
        r0, c0 = item
        rs = slice(r0, r0 + sub)
        down = jnp.dot(gate(item, us), wdn_ref[c0:c0 + cols, :], preferred_element_type=F32)
        if c0 == 0:
            acc_ref[rs, :] = down
        else:
            acc_ref[rs, :] += down
        if c0 + cols >= d_ff:
            o_ref[rs, :] = x_ref[rs, :] + _rms(acc_ref[rs, :], gpost_ref[...])
        us = us_next


def _ffn(h2, x1, wup, cw, cb, wdn, gpost, *, batch, seq):
    t, d = x1.shape
    d_ff = wdn.shape[0]
    tm = ROWS_FFN
    nt = seq // tm
    assert FFN_HALO == SUBLANES * (cw.shape[0] - 1)
    rows = lambda b, i: (b * nt + i, 0)
    halo = lambda b, i: (jnp.maximum((b * seq + i * tm) // FFN_HALO - 1, 0), 0)
    const = lambda b, i: (0, 0)
    return pl.pallas_call(
        functools.partial(_ffn_kernel, d_ff=d_ff, cols=FFN_COLS, sub=FFN_SUB),
        grid=(batch, nt),
        in_specs=[
            pl.BlockSpec((tm, d), rows),
            pl.BlockSpec((FFN_HALO, d), halo),
            pl.BlockSpec((tm, d), rows),
            pl.BlockSpec(wup.shape, const),
            pl.BlockSpec(cw.shape, const),
            pl.BlockSpec(cb.shape, const),
            pl.BlockSpec(wdn.shape, const),
            pl.BlockSpec((1, d), const),
        ],
        out_specs=pl.BlockSpec((tm, d), rows),
        out_shape=jax.ShapeDtypeStruct((t, d), F32),
        scratch_shapes=[pltpu.VMEM((FFN_HALO + tm, d), BF16), pltpu.VMEM((tm, d), F32)],
        compiler_params=pltpu.CompilerParams(
            dimension_semantics=("arbitrary", "arbitrary"), vmem_limit_bytes=VMEM_LIMIT),
        name="ffn",
    )(h2, h2, x1, wup, cw, cb, wdn, gpost)


def _block_diag(w, per_tile):
    nb, e, _ = w.shape
    w = w.reshape(nb // per_tile, per_tile, e, e)
    eye = jnp.eye(per_tile, dtype=w.dtype)
    out = jnp.einsum("tpij,pq->tpiqj", w, eye)
    return out.reshape(nb // per_tile, per_tile * e, per_tile * e)


def _to_segment_order(x):
    b, s, d = x.shape
    x = x.reshape(b, s // PERM, SUBLANES, PERM // SUBLANES, d)
    return jnp.swapaxes(x, 2, 3).reshape(b * s, d)


def _from_segment_order(x2, b, s):
    d = x2.shape[1]
    x = x2.reshape(b, s // PERM, PERM // SUBLANES, SUBLANES, d)
    return jnp.swapaxes(x, 2, 3).reshape(b, s, d)


def kernel(x, norm_mix_pre, norm_mix_post, norm_ffn_pre, norm_ffn_post, w_in, conv_m_w, conv_m_b, w_q_m, w_k_m, w_v_m, b_i_m, b_f_m, norm_m, conv_r_w, conv_r_b, w_a_r, b_a_r, w_x_r, b_x_r, lam_r, conv_s_w, conv_s_b, dt_bias_s, a_log_s, d_skip_s, norm_s, w_out, w_up, conv_f_w, conv_f_b, w_down):
    batch, seq, d = x.shape
    depth = w_in.shape[0]
    dm = conv_m_w.shape[2]
    dr = conv_r_w.shape[2]
    ds = norm_s.shape[1]
    dconv = conv_s_w.shape[2]
    hm = b_i_m.shape[1]
    hs = dt_bias_s.shape[1]
    sizes = (dm, dm, hm, hm, dr, dr, ds, dconv, hs)
    offs = [0]
    for s in sizes:
        offs.append(offs[-1] + s)
    col = lambda j: (offs[j], offs[j + 1])
    main_cols = [col(0), col(4), col(7), col(1), col(5), col(6)]
    n_main = sum(b - a for a, b in main_cols)

    row = lambda v: v.reshape(1, -1).astype(F32)
    x2 = _to_segment_order(x)
    for l in range(depth):
        wl = w_in[l]
        gate_w = jnp.concatenate(
            [wl[:, offs[2]:offs[4]], wl[:, offs[8]:offs[9]],
             jnp.zeros((d, LANES - 2 * hm - hs), F32)], axis=1)
        w_all = jnp.concatenate([wl[:, a:b] for a, b in main_cols] + [gate_w], axis=1).astype(BF16)
        gate_b = jnp.concatenate([b_i_m[l], b_f_m[l], dt_bias_s[l],
                                  jnp.zeros((LANES - 2 * hm - hs,), F32)]).reshape(1, LANES)
        alog = jnp.concatenate([jnp.zeros((GATE_DT,), F32), a_log_s[l],
                                jnp.zeros((LANES - GATE_DT - hs,), F32)]).reshape(1, LANES)
        cw = jnp.concatenate([conv_m_w[l], conv_r_w[l], conv_s_w[l]], axis=1)
        cb = jnp.concatenate([conv_m_b[l], conv_r_b[l], conv_s_b[l]]).reshape(1, -1)

        x1, h2 = _front(x2, row(norm_mix_pre[l]), w_all, cw, cb, w_q_m[l].astype(BF16), w_k_m[l].astype(BF16),
                        w_v_m[l].astype(BF16), gate_b, row(norm_m[l]), _block_diag(w_a_r[l], 4).astype(BF16),
                        _block_diag(w_x_r[l], 4).astype(BF16), row(b_a_r[l]), row(b_x_r[l]), row(lam_r[l]),
                        alog, row(jnp.repeat(d_skip_s[l], SSD_HEAD_DIM)), row(norm_s[l]),
                        w_out[l].astype(BF16), row(norm_mix_post[l]), row(norm_ffn_pre[l]),
                        batch=batch, seq=seq, n_main=n_main)
        x2 = _ffn(h2, x1, w_up[l].astype(BF16), conv_f_w[l], row(conv_f_b[l]),
                  w_down[l].astype(BF16), row(norm_ffn_post[l]), batch=batch, seq=seq)
    return _from_segment_order(x2, batch, seq)
```

```python
import functools

import jax
import jax.numpy as jnp
from jax import lax
from jax.experimental import pallas as pl
from jax.experimental.pallas import tpu as pltpu

F32 = jnp.float32
BF16 = jnp.bfloat16
EPS = 1e-6

LANES = 128
SUBLANES = 8
VMEM_LIMIT = 56 * 1024 * 1024

MLSTM_HEADS = 4
RGLRU_C = 8.0
SSD_HEAD_DIM = 64
SSD_GROUPS = 2
SSD_STATE = 128
GATE_I, GATE_F, GATE_DT = 0, 4, 8

PERM = 128
ROWS_FRONT = 256
FRONT_COLS = 256
ROWS_FFN = 512
FFN_SUB = 256
FFN_COLS = 256


def _rms(x, g):
    return x * lax.rsqrt(jnp.mean(x * x, axis=-1, keepdims=True) + EPS) * g


def _softplus(x):
    return jnp.maximum(x, 0.0) + jnp.log(1.0 + jnp.exp(-jnp.abs(x)))


def _sigmoid(x):
    return 0.5 * jnp.tanh(0.5 * x) + 0.5


def _silu(x):
    hx = 0.5 * x
    return hx * jnp.tanh(hx) + hx


def _wrap_tail(prev_tail, tail):
    out = []
    for j in range(tail.shape[0] // SUBLANES):
        rs = slice(j * SUBLANES, (j + 1) * SUBLANES)
        sub = lax.broadcasted_iota(jnp.int32, (SUBLANES, tail.shape[1]), 0)
        mixed = jnp.where(sub == SUBLANES - 1, prev_tail[rs], tail[rs])
        out.append(pltpu.roll(mixed, 1, 0))
    return jnp.concatenate(out, axis=0)


def _perm_conv(x, prev_tail, w, b):
    n = x.shape[0]
    k = w.shape[0]
    nt = SUBLANES * (k - 1)
    ext = jnp.concatenate([_wrap_tail(prev_tail, x[n - nt:]), x], axis=0)
    out = b + w[k - 1:k] * x
    for j in range(1, k):
        start = nt - SUBLANES * j
        out = out + w[k - 1 - j:k - j] * ext[start:start + n]
    return out


def _causal_mask(n):
    seg = n // SUBLANES
    row = lax.broadcasted_iota(jnp.int32, (n, n), 0)
    col = lax.broadcasted_iota(jnp.int32, (n, n), 1)
    t_row = (row & (SUBLANES - 1)) * seg + (row >> 3)
    t_col = (col & (SUBLANES - 1)) * seg + (col >> 3)
    return t_row >= t_col


def _time_cumsum(tri_bf, steps):
    c = steps.shape[1]
    hi = steps.astype(BF16)
    rest = steps - hi.astype(F32)
    mid = rest.astype(BF16)
    lo = (rest - mid.astype(F32)).astype(BF16)
    top = jnp.dot(tri_bf, jnp.concatenate([hi, mid], axis=1), preferred_element_type=F32)
    return top[:, :c] + (top[:, c:] + jnp.dot(tri_bf, lo, preferred_element_type=F32))


def _lane_pair(col_lo, col_hi, n, width):
    lane = lax.broadcasted_iota(jnp.int32, (n, width), 1)
    return jnp.where(lane < width // 2, col_lo, col_hi)


def _mlstm_project(xc, xm_bf, wqk_ref, wv_ref, h, *, n, dh):
    sl = slice(h * dh, (h + 1) * dh)
    qk = jnp.dot(xc[:, sl].astype(BF16), wqk_ref[h], preferred_element_type=F32)
    v = jnp.dot(xm_bf[:, sl], wv_ref[h], preferred_element_type=F32).astype(BF16)
    return qk[:, :dh].astype(BF16), qk[:, dh:] * (dh ** -0.5), jnp.concatenate([v, jnp.ones((n, dh), BF16)], axis=1)


def _mlstm_decay(g, cs, cs_t, g_t, causal, m_prev, h, *, n):
    bcol = cs[:, GATE_F + h:GATE_F + h + 1]
    brow = cs_t[GATE_F + h:GATE_F + h + 1, :]
    icol = g[:, GATE_I + h:GATE_I + h + 1]
    irow = g_t[GATE_I + h:GATE_I + h + 1, :]
    btot = cs[n - 1:n, GATE_F + h:GATE_F + h + 1]
    d = jnp.where(causal, bcol - brow + irow, -jnp.inf)
    inter = bcol + m_prev
    m_t = jnp.maximum(inter, jnp.max(d, axis=1, keepdims=True))
    wst = btot - bcol + icol
    m_loc = jnp.max(wst, axis=0, keepdims=True)
    m_new = jnp.maximum(btot + m_prev, m_loc)
    return dict(pmat=jnp.exp(d - m_t), e_inter=jnp.exp(inter - m_t), floor=jnp.exp(-m_t),
                ew=jnp.exp(wst - m_loc), s_prev=jnp.exp(btot + m_prev - m_new),
                s_loc=jnp.exp(m_loc - m_new), m_new=m_new)


def _mlstm_output(q, k, v_aug, dec, c_prev, *, dh):
    scores = lax.dot_general(q, k.astype(BF16), (((1,), (1,)), ((), ())),
                             preferred_element_type=F32) * dec["pmat"]
    comb = (jnp.dot(scores.astype(BF16), v_aug, preferred_element_type=F32)
            + dec["e_inter"] * jnp.dot(q, c_prev.astype(BF16), preferred_element_type=F32))
    hh = comb[:, :dh] / jnp.maximum(jnp.abs(comb[:, dh:]), dec["floor"])
    ek = (dec["ew"] * k).astype(BF16)
    c_loc = lax.dot_general(ek, v_aug, (((0,), (0,)), ((), ())), preferred_element_type=F32)
    return hh, dec["s_prev"] * c_prev + dec["s_loc"] * c_loc


def _rglru_scan(a, u, carry):
    n, d = a.shape
    tiles = n // SUBLANES
    hs, ps = [u[0:SUBLANES]], [a[0:SUBLANES]]
    for i in range(1, tiles):
        rs = slice(i * SUBLANES, (i + 1) * SUBLANES)
        hs.append(a[rs] * hs[-1] + u[rs])
        ps.append(a[rs] * ps[-1])
    gacc, pacc = hs[-1], ps[-1]
    sub = lax.broadcasted_iota(jnp.int32, (SUBLANES, d), 0)
    for s in (1, 2, 4):
        keep = sub >= s
        gacc = jnp.where(keep, pacc * pltpu.roll(gacc, s, 0) + gacc, gacc)
        pacc = jnp.where(keep, pacc * pltpu.roll(pacc, s, 0), pacc)
    seg_end = gacc + pacc * carry
    seg_in = jnp.where(sub == 0, carry, pltpu.roll(seg_end, 1, 0))
    out = [hs[i] + ps[i] * seg_in for i in range(tiles)]
    return jnp.concatenate(out, axis=0), seg_end[SUBLANES - 1:SUBLANES]


def _ssd_decay(dt, acs, acs_t, causal, h0, *, n, hd):
    pair = 2 * hd
    cols = [GATE_DT + h0, GATE_DT + h0 + 1]
    acol = [acs[:, c:c + 1] for c in cols]
    arow = [acs_t[c:c + 1, :] for c in cols]
    atot = [acs[n - 1:n, c:c + 1] for c in cols]
    return dict(
        dec=[jnp.exp(jnp.where(causal, acol[e] - arow[e], -jnp.inf)) for e in range(2)],
        dt=_lane_pair(dt[:, cols[0]:cols[0] + 1], dt[:, cols[1]:cols[1] + 1], n, pair),
        e_in=jnp.exp(_lane_pair(acol[0], acol[1], n, pair)),
        e_out=jnp.exp(_lane_pair(atot[0] - acol[0], atot[1] - acol[1], n, pair)),
        e_tot=jnp.exp(_lane_pair(atot[0], atot[1], 1, pair)))


def _ssd_output(xs_p, cbm, dec, yoff_p, dskip_p, *, n, hd):
    pair = 2 * hd
    lo = lax.broadcasted_iota(jnp.int32, (n, pair), 1) < hd
    xdt = xs_p * dec["dt"]
    ydiag = jnp.zeros((n, pair), F32)
    for e in range(2):
        half = jnp.where(lo if e == 0 else jnp.logical_not(lo), xdt, 0.0).astype(BF16)
        ydiag = ydiag + jnp.dot((cbm * dec["dec"][e]).astype(BF16), half, preferred_element_type=F32)
    return ydiag + yoff_p * dec["e_in"] + xs_p * dskip_p, (xdt * dec["e_out"]).astype(BF16)


def _front_kernel(xn_ref, xres_ref, gin_ref, win_ref, cw_ref, cb_ref, wqk_ref, wv_ref, gb_ref, gnm_ref,
                  wa_ref, wx_ref, ba_ref, bx_ref, lam_ref, alog_ref, dskip_ref, gns_ref,
                  wo_ref, gpost_ref, gffn_ref, x1_ref, h2_ref,
                  pbuf, ybuf, tail_ref, c_ref, m_ref, h_ref, s_ref, *, n, steps_per_seq, dm, dr, ds, n_main):
    step = pl.program_id(0)

    def reset_state():
        tail_ref[...] = jnp.zeros_like(tail_ref)
        c_ref[...] = jnp.zeros_like(c_ref)
        m_ref[...] = jnp.zeros_like(m_ref)
        h_ref[...] = jnp.zeros_like(h_ref)
        s_ref[...] = jnp.zeros_like(s_ref)

    @pl.when(step == 0)
    def _first():
        pbuf[1] = jnp.zeros(pbuf.shape[1:], pbuf.dtype)
        ybuf[...] = jnp.zeros_like(ybuf)
        reset_state()

    for half in range(2):
        if half == 1:
            pl.when(lax.rem(2 * step, steps_per_seq) == 0)(reset_state)
        blk = slice(half * n, (half + 1) * n)
        _front_block(xn_ref.at[blk], xres_ref.at[blk], gin_ref, win_ref, cw_ref, cb_ref, wqk_ref, wv_ref, gb_ref,
                     gnm_ref, wa_ref, wx_ref, ba_ref, bx_ref, lam_ref, alog_ref, dskip_ref, gns_ref, wo_ref,
                     gpost_ref, gffn_ref, x1_ref.at[blk], h2_ref.at[blk], pbuf.at[half], pbuf.at[1 - half], ybuf,
                     tail_ref, c_ref, m_ref, h_ref, s_ref, n=n, dm=dm, dr=dr, ds=ds, n_main=n_main)


def _front_block(xn_ref, xres_ref, gin_ref, win_ref, cw_ref, cb_ref, wqk_ref, wv_ref, gb_ref, gnm_ref,
                 wa_ref, wx_ref, ba_ref, bx_ref, lam_ref, alog_ref, dskip_ref, gns_ref,
                 wo_ref, gpost_ref, gffn_ref, x1_ref, h2_ref, p_out, p_ref,
                 ybuf, tail_ref, c_ref, m_ref, h_ref, s_ref, *, n, dm, dr, ds, n_main):
    heads_m = MLSTM_HEADS
    dh = dm // heads_m
    hd = SSD_HEAD_DIM
    pair = 2 * hd
    n_pairs = ds // pair
    pairs_per_group = n_pairs // SSD_GROUPS
    n_all = win_ref.shape[1]
    n_conv = dm + dr + 2 * ds
    o0 = n_conv
    ns = SSD_STATE

    mix = jnp.dot(ybuf[...], wo_ref[...], preferred_element_type=F32)

    h_next = _rms(xn_ref[...], gin_ref[...]).astype(BF16)
    starts = list(range(0, n_all, FRONT_COLS))
    chunks = n // PERM
    per_call = -(-len(starts) // (8 * chunks))

    def project(k=per_call):
        for _ in range(k):
            if starts:
                c0 = starts.pop(0)
                c1 = min(c0 + FRONT_COLS, n_all)
                p_out[:, c0:c1] = jnp.dot(h_next, win_ref[:, c0:c1], preferred_element_type=F32)

    causal = _causal_mask(PERM)
    tri = jnp.where(causal, 1.0, 0.0).astype(BF16)
    lane = lax.broadcasted_iota(jnp.int32, (1, LANES), 1)

    for ch in range(chunks):
        rows = slice(ch * PERM, (ch + 1) * PERM)

        g = p_ref[rows, n_main:] + gb_ref[...]
        sp = _softplus(jnp.where(lane < GATE_DT, -g, g))
        dt = sp
        steps = jnp.where(lane < GATE_DT, -sp, sp * (-jnp.exp(alog_ref[...])))
        cs = _time_cumsum(tri, steps)
        project()

        nt_rows = tail_ref.shape[0]
        x_in = p_ref[rows, :n_conv]
        conv = _perm_conv(x_in, tail_ref[...], cw_ref[...], cb_ref[...])
        tail_ref[...] = x_in[PERM - nt_rows:]
        project()
        xc_m = _silu(conv[:, :dm])
        xc_r = conv[:, dm:dm + dr]
        xs_bc = _silu(conv[:, dm + dr:])
        cs_t, g_t = cs.T, g.T

        if ch == 0:
            x1 = xres_ref[...] + _rms(mix, gpost_ref[...])
            x1_ref[...] = x1
            h2_ref[...] = _rms(x1, gffn_ref[...]).astype(h2_ref.dtype)
        project()

        xm_bf = x_in[:, :dm].astype(BF16)
        qkv = [_mlstm_project(xc_m, xm_bf, wqk_ref, wv_ref, h, n=PERM, dh=dh) for h in range(heads_m)]
        xr_bf = xc_r.astype(BF16)
        tile = wa_ref.shape[1]
        ra = [jnp.dot(xr_bf[:, j * tile:(j + 1) * tile], wa_ref[j], preferred_element_type=F32)
              for j in range(dr // tile)]
        ri = [jnp.dot(xr_bf[:, j * tile:(j + 1) * tile], wx_ref[j], preferred_element_type=F32)
              for j in range(dr // tile)]
        b_g = [xs_bc[:, ds + grp * ns:ds + (grp + 1) * ns].astype(BF16) for grp in range(SSD_GROUPS)]
        c_g = [xs_bc[:, ds + (SSD_GROUPS + grp) * ns:ds + (SSD_GROUPS + grp + 1) * ns].astype(BF16)
               for grp in range(SSD_GROUPS)]
        cbm = [lax.dot_general(c_g[grp], b_g[grp], (((1,), (1,)), ((), ())), preferred_element_type=F32)
               for grp in range(SSD_GROUPS)]

        r = _sigmoid(jnp.concatenate(ra, axis=1) + ba_ref[...])
        i = _sigmoid(jnp.concatenate(ri, axis=1) + bx_ref[...])
        a = jnp.exp((-RGLRU_C) * r * _softplus(-lam_ref[...]))
        w = 1.0 - a * a
        u = jnp.where(w > 0.0, w * lax.rsqrt(w), 0.0) * (i * xc_r)
        project()
        hseq, h_ref[...] = _rglru_scan(a, u, h_ref[...])
        yr = p_ref[rows, o0 + dm:o0 + dm + dr]
        ybuf[rows, dm:dm + dr] = (hseq * jax.nn.gelu(yr, approximate=True)).astype(ybuf.dtype)
        project()

        dec_m = [_mlstm_decay(g, cs, cs_t, g_t, causal, m_ref[h:h + 1, 0:1], h, n=PERM) for h in range(heads_m)]
        dec_s = [_ssd_decay(dt, cs, cs_t, causal, 2 * p, n=PERM, hd=hd) for p in range(n_pairs)]
        gw = pairs_per_group * pair
        yoff = [jnp.dot(c_g[grp], s_ref[:, grp * gw:(grp + 1) * gw].astype(BF16), preferred_element_type=F32)
                for grp in range(SSD_GROUPS)]
        ys, xds = [], []
        for j in range(max(heads_m, n_pairs)):
            if j < heads_m:
                sl = slice(j * dh, (j + 1) * dh)
                hh, c_ref[j] = _mlstm_output(*qkv[j], dec_m[j], c_ref[j], dh=dh)
                m_ref[j:j + 1, :] = jnp.broadcast_to(dec_m[j]["m_new"], (1, LANES))
                o = _sigmoid(p_ref[rows, o0 + j * dh:o0 + (j + 1) * dh]) * hh
                ybuf[rows, sl] = _rms(o, gnm_ref[:, sl]).astype(ybuf.dtype)
            if j % 2 == 1:
                project()
            if j < n_pairs:
                ps = slice(j * pair, (j + 1) * pair)
                grp, jg = divmod(j, pairs_per_group)
                y_p, xd_p = _ssd_output(xs_bc[:, ps], cbm[grp], dec_s[j], yoff[grp][:, jg * pair:(jg + 1) * pair],
                                        dskip_ref[:, ps], n=PERM, hd=hd)
                ys.append(y_p)
                xds.append(xd_p)
                if jg == pairs_per_group - 1:
                    gs = slice(grp * gw, (grp + 1) * gw)
                    s_loc = lax.dot_general(b_g[grp], jnp.concatenate(xds[-pairs_per_group:], axis=1),
                                            (((0,), (0,)), ((), ())), preferred_element_type=F32)
                    e_tot = jnp.concatenate([dec_s[p]["e_tot"] for p in range(j + 1 - pairs_per_group, j + 1)],
                                            axis=1)
                    s_ref[:, gs] = e_tot * s_ref[:, gs] + s_loc
        project()
        z = p_ref[rows, o0 + dm + dr:o0 + dm + dr + ds]
        ybuf[rows, dm + dr:] = _rms(jnp.concatenate(ys, axis=1) * _silu(z), gns_ref[...]).astype(ybuf.dtype)
    project(len(starts))


def _front(x2, gin, win, cw, cb, wqk, wv, gb, gnm, wa, wx, ba, bx, lam, alog, dskip, gns, wo, gpost, gffn,
           *, batch, seq, n_main):
    n = ROWS_FRONT
    t, d = x2.shape
    dm, dr, ds = gnm.shape[1], lam.shape[1], gns.shape[1]
    nt = seq // n
    nblk = batch * nt
    k = cw.shape[0]
    consts = [gin, win, cw, cb, wqk, wv, gb, gnm, wa, wx, ba, bx, lam, alog, dskip, gns, wo, gpost, gffn]
    const_spec = lambda a: pl.BlockSpec(a.shape, lambda s, _nd=a.ndim: (0,) * _nd)
    nstep = nblk // 2
    nxt = lambda s: (jnp.minimum(s, nstep - 1), 0)
    done = lambda s: (jnp.clip(s - 1, 0, nstep - 1), 0)
    return pl.pallas_call(
        functools.partial(_front_kernel, n=n, steps_per_seq=nt, dm=dm, dr=dr, ds=ds, n_main=n_main),
        grid=(nstep + 1,),
        in_specs=[pl.BlockSpec((2 * n, d), nxt), pl.BlockSpec((2 * n, d), done)] + [const_spec(a) for a in consts],
        out_specs=[pl.BlockSpec((2 * n, d), done), pl.BlockSpec((2 * n, d), done)],
        out_shape=[jax.ShapeDtypeStruct((t, d), F32), jax.ShapeDtypeStruct((t, d), BF16)],
        scratch_shapes=[
            pltpu.VMEM((2, n, win.shape[1]), F32),
            pltpu.VMEM((n, dm + dr + ds), BF16),
            pltpu.VMEM((SUBLANES * (k - 1), cw.shape[1]), F32),
            pltpu.VMEM((MLSTM_HEADS, dm // MLSTM_HEADS, 2 * dm // MLSTM_HEADS), F32),
            pltpu.VMEM((SUBLANES, LANES), F32),
            pltpu.VMEM((1, dr), F32),
            pltpu.VMEM((SSD_STATE, ds), F32),
        ],
        compiler_params=pltpu.CompilerParams(
            dimension_semantics=("arbitrary",), vmem_limit_bytes=VMEM_LIMIT),
        name="front",
    )(x2, x2, *consts)


def _ffn_kernel(h_ref, x_ref, wup_ref, cw_ref, cb_ref, wdn_ref, gpost_ref, o_ref, utail_ref, acc_ref,
                *, d_ff, cols, sub):
    t = pl.program_id(1)
    tm = h_ref.shape[0]
    nt = utail_ref.shape[0]

    @pl.when(t == 0)
    def _sequence_start():
        utail_ref[...] = jnp.zeros_like(utail_ref)

    def up(item):
        r0, c0 = item
        return [jnp.dot(h_ref[r0:r0 + sub, :], wup_ref[:, base:base + cols], preferred_element_type=F32)
                for base in (c0, d_ff + c0)]

    def gate(item, us):
        _, c0 = item
        branch = []
        for base, u in zip((c0, d_ff + c0), us):
            cs = slice(base, base + cols)
            w = cw_ref[:, cs]
            b = cb_ref[:, cs]
            prev_tail = utail_ref[:, cs]
            parts = []
            for p0 in range(0, sub, PERM):
                blk = u[p0:p0 + PERM]
                parts.append(_perm_conv(blk, prev_tail, w, b))
                prev_tail = blk[PERM - nt:]
            utail_ref[:, cs] = prev_tail
            branch.append(jnp.concatenate(parts, axis=0))
        return (jax.nn.gelu(branch[0], approximate=True) * branch[1]).astype(BF16)

    items = [(r0, c0) for r0 in range(0, tm, sub) for c0 in range(0, d_ff, cols)]
    us = up(items[0])
    for i, item in enumerate(items):
        us_next = up(items[i + 1]) if i + 1 < len(items) else None
        r0, c0 = item
        rs = slice(r0, r0 + sub)
        down = jnp.dot(gate(item, us), wdn_ref[c0:c0 + cols, :], preferred_element_type=F32)
        if c0 == 0:
            acc_ref[rs, :] = down
        else:
            acc_ref[rs, :] += down
        if c0 + cols >= d_ff:
            o_ref[rs, :] = x_ref[rs, :] + _rms(acc_ref[rs, :], gpost_ref[...])
        us = us_next


def _ffn(h2, x1, wup, cw, cb, wdn, gpost, layer, *, batch, seq):
    t, d = x1.shape
    d_ff = wdn.shape[1]
    tm = ROWS_FFN
    nt = seq // tm
    rows = lambda b, i: (b * nt + i, 0)
    const = lambda b, i: (0, 0)
    pick = lambda b, i: (layer, 0, 0)
    return pl.pallas_call(
        functools.partial(_ffn_kernel, d_ff=d_ff, cols=FFN_COLS, sub=FFN_SUB),
        grid=(batch, nt),
        in_specs=[
            pl.BlockSpec((tm, d), rows),
            pl.BlockSpec((tm, d), rows),
            pl.BlockSpec((None,) + wup.shape[1:], pick),
            pl.BlockSpec(cw.shape, const),
            pl.BlockSpec(cb.shape, const),
            pl.BlockSpec((None,) + wdn.shape[1:], pick),
            pl.BlockSpec((1, d), const),
        ],
        out_specs=pl.BlockSpec((tm, d), rows),
        out_shape=jax.ShapeDtypeStruct((t, d), F32),
        scratch_shapes=[pltpu.VMEM((SUBLANES * (cw.shape[0] - 1), 2 * d_ff), F32),
                        pltpu.VMEM((tm, d), F32)],
        compiler_params=pltpu.CompilerParams(
            dimension_semantics=("arbitrary", "arbitrary"), vmem_limit_bytes=VMEM_LIMIT),
        name="ffn",
    )(h2, x1, wup, cw, cb, wdn, gpost)


def _block_diag(w, per_tile):
    nb, e, _ = w.shape
    w = w.reshape(nb // per_tile, per_tile, e, e)
    eye = jnp.eye(per_tile, dtype=w.dtype)
    out = jnp.einsum("tpij,pq->tpiqj", w, eye)
    return out.reshape(nb // per_tile, per_tile * e, per_tile * e)


def _to_segment_order(x):
    b, s, d = x.shape
    x = x.reshape(b, s // PERM, SUBLANES, PERM // SUBLANES, d)
    return jnp.swapaxes(x, 2, 3).reshape(b * s, d)


def _from_segment_order(x2, b, s):
    d = x2.shape[1]
    x = x2.reshape(b, s // PERM, PERM // SUBLANES, SUBLANES, d)
    return jnp.swapaxes(x, 2, 3).reshape(b, s, d)


def kernel(x, norm_mix_pre, norm_mix_post, norm_ffn_pre, norm_ffn_post, w_in, conv_m_w, conv_m_b, w_q_m, w_k_m, w_v_m, b_i_m, b_f_m, norm_m, conv_r_w, conv_r_b, w_a_r, b_a_r, w_x_r, b_x_r, lam_r, conv_s_w, conv_s_b, dt_bias_s, a_log_s, d_skip_s, norm_s, w_out, w_up, conv_f_w, conv_f_b, w_down):
    batch, seq, d = x.shape
    depth = w_in.shape[0]
    dm = conv_m_w.shape[2]
    dr = conv_r_w.shape[2]
    ds = norm_s.shape[1]
    dconv = conv_s_w.shape[2]
    hm = b_i_m.shape[1]
    hs = dt_bias_s.shape[1]
    sizes = (dm, dm, hm, hm, dr, dr, ds, dconv, hs)
    offs = [0]
    for s in sizes:
        offs.append(offs[-1] + s)
    col = lambda j: (offs[j], offs[j + 1])
    main_cols = [col(0), col(4), col(7), col(1), col(5), col(6)]
    n_main = sum(b - a for a, b in main_cols)

    row = lambda v: v.reshape(1, -1).astype(F32)
    x2 = _to_segment_order(x)
    w_in_bf, w_out_bf = w_in.astype(BF16), w_out.astype(BF16)
    w_up_bf, w_down_bf = w_up.astype(BF16), w_down.astype(BF16)
    for l in range(depth):
        wl = w_in_bf[l]
        gate_w = jnp.concatenate(
            [wl[:, offs[2]:offs[4]], wl[:, offs[8]:offs[9]],
             jnp.zeros((d, LANES - 2 * hm - hs), BF16)], axis=1)
        w_all = jnp.concatenate([wl[:, a:b] for a, b in main_cols] + [gate_w], axis=1)
        gate_b = jnp.concatenate([b_i_m[l], b_f_m[l], dt_bias_s[l],
                                  jnp.zeros((LANES - 2 * hm - hs,), F32)]).reshape(1, LANES)
        alog = jnp.concatenate([jnp.zeros((GATE_DT,), F32), a_log_s[l],
                                jnp.zeros((LANES - GATE_DT - hs,), F32)]).reshape(1, LANES)
        cw = jnp.concatenate([conv_m_w[l], conv_r_w[l], conv_s_w[l]], axis=1)
        cb = jnp.concatenate([conv_m_b[l], conv_r_b[l], conv_s_b[l]]).reshape(1, -1)

        x1, h2 = _front(x2, row(norm_mix_pre[l]), w_all, cw, cb, jnp.concatenate([w_q_m[l], w_k_m[l]], axis=-1).astype(BF16),
                        w_v_m[l].astype(BF16), gate_b, row(norm_m[l]), _block_diag(w_a_r[l], 4).astype(BF16),
                        _block_diag(w_x_r[l], 4).astype(BF16), row(b_a_r[l]), row(b_x_r[l]), row(lam_r[l]),
                        alog, row(jnp.repeat(d_skip_s[l], SSD_HEAD_DIM)), row(norm_s[l]),
                        w_out_bf[l], row(norm_mix_post[l]), row(norm_ffn_pre[l]),
                        batch=batch, seq=seq, n_main=n_main)
        x2 = _ffn(h2, x1, w_up_bf, conv_f_w[l], row(conv_f_b[l]), w_down_bf, row(norm_ffn_post[l]), l,
                  batch=batch, seq=seq)
    return _from_segment_order(x2, batch, seq)
```

```python
import functools

import jax
import jax.numpy as jnp
from jax import lax
from jax.experimental import pallas as pl
from jax.experimental.pallas import tpu as pltpu

F32 = jnp.float32
BF16 = jnp.bfloat16
EPS = 1e-6

LANES = 128
SUBLANES = 8
VMEM_LIMIT = 56 * 1024 * 1024

MLSTM_HEADS = 4
RGLRU_C = 8.0
SSD_HEAD_DIM = 64
SSD_GROUPS = 2
SSD_STATE = 128
GATE_I, GATE_F, GATE_DT = 0, 4, 8

PERM = 128
ROWS_FRONT = 256
FRONT_COLS = 256
ROWS_FFN = 512
FFN_SUB = 256
ROWS_CAST = 256
FFN_COLS = 256
FFN_HALO = 16


def _rms(x, g):
    return x * lax.rsqrt(jnp.mean(x * x, axis=-1, keepdims=True) + EPS) * g


def _softplus(x):
    return jnp.maximum(x, 0.0) + jnp.log(1.0 + jnp.exp(-jnp.abs(x)))


def _sigmoid(x):
    return 0.5 * jnp.tanh(0.5 * x) + 0.5


def _silu(x):
    hx = 0.5 * x
    return hx * jnp.tanh(hx) + hx


def _wrap_tail(prev_tail, tail):
    out = []
    for j in range(tail.shape[0] // SUBLANES):
        rs = slice(j * SUBLANES, (j + 1) * SUBLANES)
        sub = lax.broadcasted_iota(jnp.int32, (SUBLANES, tail.shape[1]), 0)
        mixed = jnp.where(sub == SUBLANES - 1, prev_tail[rs], tail[rs])
        out.append(pltpu.roll(mixed, 1, 0))
    return jnp.concatenate(out, axis=0)


def _perm_conv(x, prev_tail, w, b):
    n = x.shape[0]
    k = w.shape[0]
    nt = SUBLANES * (k - 1)
    ext = jnp.concatenate([_wrap_tail(prev_tail, x[n - nt:]), x], axis=0)
    out = b + w[k - 1:k] * x
    for j in range(1, k):
        start = nt - SUBLANES * j
        out = out + w[k - 1 - j:k - j] * ext[start:start + n]
    return out


def _causal_mask(n):
    seg = n // SUBLANES
    row = lax.broadcasted_iota(jnp.int32, (n, n), 0)
    col = lax.broadcasted_iota(jnp.int32, (n, n), 1)
    t_row = (row & (SUBLANES - 1)) * seg + (row >> 3)
    t_col = (col & (SUBLANES - 1)) * seg + (col >> 3)
    return t_row >= t_col


def _time_cumsum(tri_bf, steps):
    c = steps.shape[1]
    hi = steps.astype(BF16)
    rest = steps - hi.astype(F32)
    mid = rest.astype(BF16)
    lo = (rest - mid.astype(F32)).astype(BF16)
    top = jnp.dot(tri_bf, jnp.concatenate([hi, mid], axis=1), preferred_element_type=F32)
    return top[:, :c] + (top[:, c:] + jnp.dot(tri_bf, lo, preferred_element_type=F32))


def _lane_pair(col_lo, col_hi, n, width):
    lane = lax.broadcasted_iota(jnp.int32, (n, width), 1)
    return jnp.where(lane < width // 2, col_lo, col_hi)


def _mlstm_project(xc, xm_bf, wqk_ref, wv_ref, h, *, n, dh):
    sl = slice(h * dh, (h + 1) * dh)
    qk = jnp.dot(xc[:, sl].astype(BF16), wqk_ref[h], preferred_element_type=F32)
    v = jnp.dot(xm_bf[:, sl], wv_ref[h], preferred_element_type=F32).astype(BF16)
    return qk[:, :dh].astype(BF16), qk[:, dh:] * (dh ** -0.5), jnp.concatenate([v, jnp.ones((n, dh), BF16)], axis=1)


def _mlstm_decay(g, cs, cs_t, g_t, causal, m_prev, h, *, n):
    bcol = cs[:, GATE_F + h:GATE_F + h + 1]
    brow = cs_t[GATE_F + h:GATE_F + h + 1, :]
    icol = g[:, GATE_I + h:GATE_I + h + 1]
    irow = g_t[GATE_I + h:GATE_I + h + 1, :]
    btot = cs[n - 1:n, GATE_F + h:GATE_F + h + 1]
    d = jnp.where(causal, bcol - brow + irow, -jnp.inf)
    inter = bcol + m_prev
    m_t = jnp.maximum(inter, jnp.max(d, axis=1, keepdims=True))
    wst = btot - bcol + icol
    m_loc = jnp.max(wst, axis=0, keepdims=True)
    m_new = jnp.maximum(btot + m_prev, m_loc)
    return dict(pmat=jnp.exp(d - m_t), e_inter=jnp.exp(inter - m_t), floor=jnp.exp(-m_t),
                ew=jnp.exp(wst - m_loc), s_prev=jnp.exp(btot + m_prev - m_new),
                s_loc=jnp.exp(m_loc - m_new), m_new=m_new)


def _mlstm_output(q, k, v_aug, dec, c_prev, *, dh):
    scores = lax.dot_general(q, k.astype(BF16), (((1,), (1,)), ((), ())),
                             preferred_element_type=F32) * dec["pmat"]
    comb = (jnp.dot(scores.astype(BF16), v_aug, preferred_element_type=F32)
            + dec["e_inter"] * jnp.dot(q, c_prev.astype(BF16), preferred_element_type=F32))
    hh = comb[:, :dh] / jnp.maximum(jnp.abs(comb[:, dh:]), dec["floor"])
    ek = (dec["ew"] * k).astype(BF16)
    c_loc = lax.dot_general(ek, v_aug, (((0,), (0,)), ((), ())), preferred_element_type=F32)
    return hh, dec["s_prev"] * c_prev + dec["s_loc"] * c_loc


def _rglru_scan(a, u, carry):
    n, d = a.shape
    tiles = n // SUBLANES
    hs, ps = [u[0:SUBLANES]], [a[0:SUBLANES]]
    for i in range(1, tiles):
        rs = slice(i * SUBLANES, (i + 1) * SUBLANES)
        hs.append(a[rs] * hs[-1] + u[rs])
        ps.append(a[rs] * ps[-1])
    gacc, pacc = hs[-1], ps[-1]
    sub = lax.broadcasted_iota(jnp.int32, (SUBLANES, d), 0)
    for s in (1, 2, 4):
        keep = sub >= s
        gacc = jnp.where(keep, pacc * pltpu.roll(gacc, s, 0) + gacc, gacc)
        pacc = jnp.where(keep, pacc * pltpu.roll(pacc, s, 0), pacc)
    seg_end = gacc + pacc * carry
    seg_in = jnp.where(sub == 0, carry, pltpu.roll(seg_end, 1, 0))
    out = [hs[i] + ps[i] * seg_in for i in range(tiles)]
    return jnp.concatenate(out, axis=0), seg_end[SUBLANES - 1:SUBLANES]


def _ssd_decay(dt, acs, acs_t, causal, h0, *, n, hd):
    pair = 2 * hd
    cols = [GATE_DT + h0, GATE_DT + h0 + 1]
    acol = [acs[:, c:c + 1] for c in cols]
    arow = [acs_t[c:c + 1, :] for c in cols]
    atot = [acs[n - 1:n, c:c + 1] for c in cols]
    return dict(
        dec=[jnp.exp(jnp.where(causal, acol[e] - arow[e], -jnp.inf)) for e in range(2)],
        dt=_lane_pair(dt[:, cols[0]:cols[0] + 1], dt[:, cols[1]:cols[1] + 1], n, pair),
        e_in=jnp.exp(_lane_pair(acol[0], acol[1], n, pair)),
        e_out=jnp.exp(_lane_pair(atot[0] - acol[0], atot[1] - acol[1], n, pair)),
        e_tot=jnp.exp(_lane_pair(atot[0], atot[1], 1, pair)))


def _ssd_output(xs_p, cbm, dec, yoff_p, dskip_p, *, n, hd):
    pair = 2 * hd
    lo = lax.broadcasted_iota(jnp.int32, (n, pair), 1) < hd
    xdt = xs_p * dec["dt"]
    ydiag = jnp.zeros((n, pair), F32)
    for e in range(2):
        half = jnp.where(lo if e == 0 else jnp.logical_not(lo), xdt, 0.0).astype(BF16)
        ydiag = ydiag + jnp.dot((cbm * dec["dec"][e]).astype(BF16), half, preferred_element_type=F32)
    return ydiag + yoff_p * dec["e_in"] + xs_p * dskip_p, (xdt * dec["e_out"]).astype(BF16)


def _front_kernel(xn_ref, xres_ref, gin_ref, win_ref, cw_ref, cb_ref, wqk_ref, wv_ref, gb_ref, gnm_ref,
                  wa_ref, wx_ref, ba_ref, bx_ref, lam_ref, alog_ref, dskip_ref, gns_ref,
                  wo_ref, gpost_ref, gffn_ref, x1_ref, h2_ref,
                  pbuf, ybuf, tail_ref, c_ref, m_ref, h_ref, s_ref, *, n, steps_per_seq, dm, dr, ds, n_main):
    step = pl.program_id(0)

    def reset_state():
        tail_ref[...] = jnp.zeros_like(tail_ref)
        c_ref[...] = jnp.zeros_like(c_ref)
        m_ref[...] = jnp.zeros_like(m_ref)
        h_ref[...] = jnp.zeros_like(h_ref)
        s_ref[...] = jnp.zeros_like(s_ref)

    @pl.when(step == 0)
    def _first():
        pbuf[1] = jnp.zeros(pbuf.shape[1:], pbuf.dtype)
        ybuf[...] = jnp.zeros_like(ybuf)
        reset_state()

    for half in range(2):
        if half == 1:
            pl.when(lax.rem(2 * step, steps_per_seq) == 0)(reset_state)
        blk = slice(half * n, (half + 1) * n)
        _front_block(xn_ref.at[blk], xres_ref, gin_ref, win_ref, cw_ref, cb_ref, wqk_ref, wv_ref, gb_ref,
                     gnm_ref, wa_ref, wx_ref, ba_ref, bx_ref, lam_ref, alog_ref, dskip_ref, gns_ref, wo_ref,
                     gpost_ref, gffn_ref, x1_ref, h2_ref, pbuf.at[half], pbuf.at[1 - half], ybuf,
                     ybuf.at[slice((1 - half) * n, (2 - half) * n)],
                     tail_ref, c_ref, m_ref, h_ref, s_ref, n=n, dm=dm, dr=dr, ds=ds, n_main=n_main,
                     out_project=(half == 1))


def _front_block(xn_ref, xres_ref, gin_ref, win_ref, cw_ref, cb_ref, wqk_ref, wv_ref, gb_ref, gnm_ref,
                 wa_ref, wx_ref, ba_ref, bx_ref, lam_ref, alog_ref, dskip_ref, gns_ref,
                 wo_ref, gpost_ref, gffn_ref, x1_ref, h2_ref, p_out, p_ref,
                 y_all, ybuf, tail_ref, c_ref, m_ref, h_ref, s_ref, *, n, dm, dr, ds, n_main, out_project):
    heads_m = MLSTM_HEADS
    dh = dm // heads_m
    hd = SSD_HEAD_DIM
    pair = 2 * hd
    n_pairs = ds // pair
    pairs_per_group = n_pairs // SSD_GROUPS
    n_all = win_ref.shape[1]
    n_conv = dm + dr + 2 * ds
    o0 = n_conv
    ns = SSD_STATE

    if out_project:
        mix = jnp.dot(y_all[...], wo_ref[...], preferred_element_type=F32)

    h_next = _rms(xn_ref[...], gin_ref[...]).astype(BF16)
    starts = list(range(0, n_all, FRONT_COLS))
    chunks = n // PERM
    per_call = -(-len(starts) // (8 * chunks))

    def project(k=per_call):
        for _ in range(k):
            if starts:
                c0 = starts.pop(0)
                c1 = min(c0 + FRONT_COLS, n_all)
                p_out[:, c0:c1] = jnp.dot(h_next, win_ref[:, c0:c1], preferred_element_type=F32)

    causal = _causal_mask(PERM)
    tri = jnp.where(causal, 1.0, 0.0).astype(BF16)
    lane = lax.broadcasted_iota(jnp.int32, (1, LANES), 1)

    for ch in range(chunks):
        rows = slice(ch * PERM, (ch + 1) * PERM)

        g = p_ref[rows, n_main:] + gb_ref[...]
        sp = _softplus(jnp.where(lane < GATE_DT, -g, g))
        dt = sp
        steps = jnp.where(lane < GATE_DT, -sp, sp * (-jnp.exp(alog_ref[...])))
        cs = _time_cumsum(tri, steps)
        project()

        nt_rows = tail_ref.shape[0]
        x_in = p_ref[rows, :n_conv]
        conv = _perm_conv(x_in, tail_ref[...], cw_ref[...], cb_ref[...])
        tail_ref[...] = x_in[PERM - nt_rows:]
        project()
        xc_m = _silu(conv[:, :dm])
        xc_r = conv[:, dm:dm + dr]
        xs_bc = _silu(conv[:, dm + dr:])
        cs_t, g_t = cs.T, g.T

        if ch == 0 and out_project:
            x1 = xres_ref[...] + _rms(mix, gpost_ref[...])
            x1_ref[...] = x1
            h2_ref[...] = _rms(x1, gffn_ref[...]).astype(h2_ref.dtype)
        project()

        xm_bf = x_in[:, :dm].astype(BF16)
        qkv = [_mlstm_project(xc_m, xm_bf, wqk_ref, wv_ref, h, n=PERM, dh=dh) for h in range(heads_m)]
        xr_bf = xc_r.astype(BF16)
        tile = wa_ref.shape[1]
        ra = [jnp.dot(xr_bf[:, j * tile:(j + 1) * tile], wa_ref[j], preferred_element_type=F32)
              for j in range(dr // tile)]
        ri = [jnp.dot(xr_bf[:, j * tile:(j + 1) * tile], wx_ref[j], preferred_element_type=F32)
              for j in range(dr // tile)]
        b_g = [xs_bc[:, ds + grp * ns:ds + (grp + 1) * ns].astype(BF16) for grp in range(SSD_GROUPS)]
        c_g = [xs_bc[:, ds + (SSD_GROUPS + grp) * ns:ds + (SSD_GROUPS + grp + 1) * ns].astype(BF16)
               for grp in range(SSD_GROUPS)]
        cbm = [lax.dot_general(c_g[grp], b_g[grp], (((1,), (1,)), ((), ())), preferred_element_type=F32)
               for grp in range(SSD_GROUPS)]

        r = _sigmoid(jnp.concatenate(ra, axis=1) + ba_ref[...])
        i = _sigmoid(jnp.concatenate(ri, axis=1) + bx_ref[...])
        a = jnp.exp((-RGLRU_C) * r * _softplus(-lam_ref[...]))
        w = 1.0 - a * a
        u = jnp.where(w > 0.0, w * lax.rsqrt(w), 0.0) * (i * xc_r)
        project()
        hseq, h_ref[...] = _rglru_scan(a, u, h_ref[...])
        yr = p_ref[rows, o0 + dm:o0 + dm + dr]
        ybuf[rows, dm:dm + dr] = (hseq * jax.nn.gelu(yr, approximate=True)).astype(ybuf.dtype)
        project()

        dec_m = [_mlstm_decay(g, cs, cs_t, g_t, causal, m_ref[h:h + 1, 0:1], h, n=PERM) for h in range(heads_m)]
        dec_s = [_ssd_decay(dt, cs, cs_t, causal, 2 * p, n=PERM, hd=hd) for p in range(n_pairs)]
        gw = pairs_per_group * pair
        yoff = [jnp.dot(c_g[grp], s_ref[:, grp * gw:(grp + 1) * gw].astype(BF16), preferred_element_type=F32)
                for grp in range(SSD_GROUPS)]
        ys, xds = [], []
        for j in range(max(heads_m, n_pairs)):
            if j < heads_m:
                sl = slice(j * dh, (j + 1) * dh)
                hh, c_ref[j] = _mlstm_output(*qkv[j], dec_m[j], c_ref[j], dh=dh)
                m_ref[j:j + 1, :] = jnp.broadcast_to(dec_m[j]["m_new"], (1, LANES))
                o = _sigmoid(p_ref[rows, o0 + j * dh:o0 + (j + 1) * dh]) * hh
                ybuf[rows, sl] = _rms(o, gnm_ref[:, sl]).astype(ybuf.dtype)
            if j % 2 == 1:
                project()
            if j < n_pairs:
                ps = slice(j * pair, (j + 1) * pair)
                grp, jg = divmod(j, pairs_per_group)
                y_p, xd_p = _ssd_output(xs_bc[:, ps], cbm[grp], dec_s[j], yoff[grp][:, jg * pair:(jg + 1) * pair],
                                        dskip_ref[:, ps], n=PERM, hd=hd)
                ys.append(y_p)
                xds.append(xd_p)
                if jg == pairs_per_group - 1:
                    gs = slice(grp * gw, (grp + 1) * gw)
                    s_loc = lax.dot_general(b_g[grp], jnp.concatenate(xds[-pairs_per_group:], axis=1),
                                            (((0,), (0,)), ((), ())), preferred_element_type=F32)
                    e_tot = jnp.concatenate([dec_s[p]["e_tot"] for p in range(j + 1 - pairs_per_group, j + 1)],
                                            axis=1)
                    s_ref[:, gs] = e_tot * s_ref[:, gs] + s_loc
        project()
        z = p_ref[rows, o0 + dm + dr:o0 + dm + dr + ds]
        ybuf[rows, dm + dr:] = _rms(jnp.concatenate(ys, axis=1) * _silu(z), gns_ref[...]).astype(ybuf.dtype)
    project(len(starts))


def _front(x2, gin, win, cw, cb, wqk, wv, gb, gnm, wa, wx, ba, bx, lam, alog, dskip, gns, wo, gpost, gffn,
           *, batch, seq, n_main):
    n = ROWS_FRONT
    t, d = x2.shape
    dm, dr, ds = gnm.shape[1], lam.shape[1], gns.shape[1]
    nt = seq // n
    nblk = batch * nt
    k = cw.shape[0]
    consts = [gin, win, cw, cb, wqk, wv, gb, gnm, wa, wx, ba, bx, lam, alog, dskip, gns, wo, gpost, gffn]
    const_spec = lambda a: pl.BlockSpec(a.shape, lambda s, _nd=a.ndim: (0,) * _nd)
    nstep = nblk // 2
    nxt = lambda s: (jnp.minimum(s, nstep - 1), 0)
    done = lambda s: (jnp.clip(s - 1, 0, nstep - 1), 0)
    return pl.pallas_call(
        functools.partial(_front_kernel, n=n, steps_per_seq=nt, dm=dm, dr=dr, ds=ds, n_main=n_main),
        grid=(nstep + 1,),
        in_specs=[pl.BlockSpec((2 * n, d), nxt), pl.BlockSpec((2 * n, d), done)] + [const_spec(a) for a in consts],
        out_specs=[pl.BlockSpec((2 * n, d), done), pl.BlockSpec((2 * n, d), done)],
        out_shape=[jax.ShapeDtypeStruct((t, d), F32), jax.ShapeDtypeStruct((t, d), BF16)],
        scratch_shapes=[
            pltpu.VMEM((2, n, win.shape[1]), F32),
            pltpu.VMEM((2 * n, dm + dr + ds), BF16),
            pltpu.VMEM((SUBLANES * (k - 1), cw.shape[1]), F32),
            pltpu.VMEM((MLSTM_HEADS, dm // MLSTM_HEADS, 2 * dm // MLSTM_HEADS), F32),
            pltpu.VMEM((SUBLANES, LANES), F32),
            pltpu.VMEM((1, dr), F32),
            pltpu.VMEM((SSD_STATE, ds), F32),
        ],
        compiler_params=pltpu.CompilerParams(
            dimension_semantics=("arbitrary",), vmem_limit_bytes=VMEM_LIMIT),
        name="front",
    )(x2, x2, *consts)


def _ffn_kernel(h_ref, hh_ref, x_ref, wup_ref, cw_ref, cb_ref, wdn_ref, gpost_ref, o_ref, he_ref, acc_ref,
                *, d_ff, cols, sub):
    t = pl.program_id(1)
    tm = h_ref.shape[0]
    halo = FFN_HALO
    he_ref[0:halo, :] = jnp.where(t > 0, hh_ref[...], jnp.zeros_like(hh_ref))
    he_ref[halo:, :] = h_ref[...]

    def up(item):
        r0, c0 = item
        he = he_ref[r0:r0 + halo + sub, :]
        return [jnp.dot(he, wup_ref[:, base:base + cols], preferred_element_type=F32)
                for base in (c0, d_ff + c0)]

    def gate(item, us):
        _, c0 = item
        branch = []
        for base, u in zip((c0, d_ff + c0), us):
            cs = slice(base, base + cols)
            w = cw_ref[:, cs]
            b = cb_ref[:, cs]
            parts = []
            for p0 in range(0, sub, PERM):
                blk = u[halo + p0:halo + p0 + PERM]
                parts.append(_perm_conv(blk, u[p0:p0 + halo], w, b))
            branch.append(jnp.concatenate(parts, axis=0))
        return (jax.nn.gelu(branch[0], approximate=True) * branch[1]).astype(BF16)

    items = [(r0, c0) for r0 in range(0, tm, sub) for c0 in range(0, d_ff, cols)]
    us = up(items[0])
    for i, item in enumerate(items):
        us_next = up(items[i + 1]) if i + 1 < len(items) else None
        r0, c0 = item
        rs = slice(r0, r0 + sub)
        down = jnp.dot(gate(item, us), wdn_ref[c0:c0 + cols, :], preferred_element_type=F32)
        if c0 == 0:
            acc_ref[rs, :] = down
        else:
            acc_ref[rs, :] += down
        if c0 + cols >= d_ff:
            o_ref[rs, :] = x_ref[rs, :] + _rms(acc_ref[rs, :], gpost_ref[...])
        us = us_next


def _ffn(h2, x1, wup, cw, cb, wdn, gpost, *, batch, seq):
    t, d = x1.shape
    d_ff = wdn.shape[0]
    tm = ROWS_FFN
    nt = seq // tm
    assert FFN_HALO == SUBLANES * (cw.shape[0] - 1)
    rows = lambda b, i: (b * nt + i, 0)
    halo = lambda b, i: (jnp.maximum((b * seq + i * tm) // FFN_HALO - 1, 0), 0)
    const = lambda b, i: (0, 0)
    return pl.pallas_call(
        functools.partial(_ffn_kernel, d_ff=d_ff, cols=FFN_COLS, sub=FFN_SUB),
        grid=(batch, nt),
        in_specs=[
            pl.BlockSpec((tm, d), rows),
            pl.BlockSpec((FFN_HALO, d), halo),
            pl.BlockSpec((tm, d), rows),
            pl.BlockSpec(wup.shape, const),
            pl.BlockSpec(cw.shape, const),
            pl.BlockSpec(cb.shape, const),
            pl.BlockSpec(wdn.shape, const),
            pl.BlockSpec((1, d), const),
        ],
        out_specs=pl.BlockSpec((tm, d), rows),
        out_shape=jax.ShapeDtypeStruct((t, d), F32),
        scratch_shapes=[pltpu.VMEM((FFN_HALO + tm, d), BF16), pltpu.VMEM((tm, d), F32)],
        compiler_params=pltpu.CompilerParams(
            dimension_semantics=("arbitrary", "arbitrary"), vmem_limit_bytes=VMEM_LIMIT),
        name="ffn",
    )(h2, h2, x1, wup, cw, cb, wdn, gpost)


def _cast_kernel(w_ref, o_ref):
    o_ref[...] = w_ref[...].astype(o_ref.dtype)


def _layer_bf16(w, layer):
    _, r, c = w.shape
    tr = ROWS_CAST
    return pl.pallas_call(
        _cast_kernel,
        grid=(r // tr,),
        in_specs=[pl.BlockSpec((None, tr, c), lambda i: (layer, i, 0))],
        out_specs=pl.BlockSpec((tr, c), lambda i: (i, 0)),
        out_shape=jax.ShapeDtypeStruct((r, c), BF16),
        compiler_params=pltpu.CompilerParams(
            dimension_semantics=("arbitrary",), vmem_limit_bytes=VMEM_LIMIT),
        name="cast",
    )(w)


def _block_diag(w, per_tile):
    nb, e, _ = w.shape
    w = w.reshape(nb // per_tile, per_tile, e, e)
    eye = jnp.eye(per_tile, dtype=w.dtype)
    out = jnp.einsum("tpij,pq->tpiqj", w, eye)
    return out.reshape(nb // per_tile, per_tile * e, per_tile * e)


def _to_segment_order(x):
    b, s, d = x.shape
    x = x.reshape(b, s // PERM, SUBLANES, PERM // SUBLANES, d)
    return jnp.swapaxes(x, 2, 3).reshape(b * s, d)


def _from_segment_order(x2, b, s):
    d = x2.shape[1]
    x = x2.reshape(b, s // PERM, PERM // SUBLANES, SUBLANES, d)
    return jnp.swapaxes(x, 2, 3).reshape(b, s, d)


def kernel(x, norm_mix_pre, norm_mix_post, norm_ffn_pre, norm_ffn_post, w_in, conv_m_w, conv_m_b, w_q_m, w_k_m, w_v_m, b_i_m, b_f_m, norm_m, conv_r_w, conv_r_b, w_a_r, b_a_r, w_x_r, b_x_r, lam_r, conv_s_w, conv_s_b, dt_bias_s, a_log_s, d_skip_s, norm_s, w_out, w_up, conv_f_w, conv_f_b, w_down):
    batch, seq, d = x.shape
    depth = w_in.shape[0]
    dm = conv_m_w.shape[2]
    dr = conv_r_w.shape[2]
    ds = norm_s.shape[1]
    dconv = conv_s_w.shape[2]
    hm = b_i_m.shape[1]
    hs = dt_bias_s.shape[1]
    sizes = (dm, dm, hm, hm, dr, dr, ds, dconv, hs)
    offs = [0]
    for s in sizes:
        offs.append(offs[-1] + s)
    col = lambda j: (offs[j], offs[j + 1])
    main_cols = [col(0), col(4), col(7), col(1), col(5), col(6)]
    n_main = sum(b - a for a, b in main_cols)

    row = lambda v: v.reshape(1, -1).astype(F32)
    x2 = _to_segment_order(x)
    w_in_bf = w_in.astype(BF16)
    for l in range(depth):
        wl = w_in_bf[l]
        gate_w = jnp.concatenate(
            [wl[:, offs[2]:offs[4]], wl[:, offs[8]:offs[9]],
             jnp.zeros((d, LANES - 2 * hm - hs), BF16)], axis=1)
        w_all = jnp.concatenate([wl[:, a:b] for a, b in main_cols] + [gate_w], axis=1)
        gate_b = jnp.concatenate([b_i_m[l], b_f_m[l], dt_bias_s[l],
                                  jnp.zeros((LANES - 2 * hm - hs,), F32)]).reshape(1, LANES)
        alog = jnp.concatenate([jnp.zeros((GATE_DT,), F32), a_log_s[l],
                                jnp.zeros((LANES - GATE_DT - hs,), F32)]).reshape(1, LANES)
        cw = jnp.concatenate([conv_m_w[l], conv_r_w[l], conv_s_w[l]], axis=1)
        cb = jnp.concatenate([conv_m_b[l], conv_r_b[l], conv_s_b[l]]).reshape(1, -1)

        x1, h2 = _front(x2, row(norm_mix_pre[l]), w_all, cw, cb, jnp.concatenate([w_q_m[l], w_k_m[l]], axis=-1).astype(BF16),
                        w_v_m[l].astype(BF16), gate_b, row(norm_m[l]), _block_diag(w_a_r[l], 4).astype(BF16),
                        _block_diag(w_x_r[l], 4).astype(BF16), row(b_a_r[l]), row(b_x_r[l]), row(lam_r[l]),
                        alog, row(jnp.repeat(d_skip_s[l], SSD_HEAD_DIM)), row(norm_s[l]),
                        _layer_bf16(w_out, l), row(norm_mix_post[l]), row(norm_ffn_pre[l]),
                        batch=batch, seq=seq, n_main=n_main)
        x2 = _ffn(h2, x1, _layer_bf16(w_up, l), conv_f_w[l], row(conv_f_b[l]),
                  _layer_bf16(w_down, l), row(norm_ffn_post[l]), batch=batch, seq=seq)
    return _from_segment_order(x2, batch, seq)
```

```python
import functools

import jax
import jax.numpy as jnp
from jax import lax
from jax.experimental import pallas as pl
from jax.experimental.pallas import tpu as pltpu

F32 = jnp.float32
BF16 = jnp.bfloat16
EPS = 1e-6

LANES = 128
SUBLANES = 8
VMEM_LIMIT = 56 * 1024 * 1024

MLSTM_HEADS = 4
RGLRU_C = 8.0
SSD_HEAD_DIM = 64
SSD_GROUPS = 2
SSD_STATE = 128
GATE_I, GATE_F, GATE_DT = 0, 4, 8

PERM = 128
ROWS_FRONT = 256
FRONT_COLS = 256
ROWS_FFN = 512
FFN_SUB = 256
ROWS_CAST = 64
FFN_COLS = 256
FFN_HALO = 16


def _rms(x, g):
    return x * lax.rsqrt(jnp.mean(x * x, axis=-1, keepdims=True) + EPS) * g


def _softplus(x):
    return jnp.maximum(x, 0.0) + jnp.log(1.0 + jnp.exp(-jnp.abs(x)))


def _sigmoid(x):
    return 0.5 * jnp.tanh(0.5 * x) + 0.5


def _silu(x):
    hx = 0.5 * x
    return hx * jnp.tanh(hx) + hx


def _wrap_tail(prev_tail, tail):
    out = []
    for j in range(tail.shape[0] // SUBLANES):
        rs = slice(j * SUBLANES, (j + 1) * SUBLANES)
        sub = lax.broadcasted_iota(jnp.int32, (SUBLANES, tail.shape[1]), 0)
        mixed = jnp.where(sub == SUBLANES - 1, prev_tail[rs], tail[rs])
        out.append(pltpu.roll(mixed, 1, 0))
    return jnp.concatenate(out, axis=0)


def _perm_conv(x, prev_tail, w, b):
    n = x.shape[0]
    k = w.shape[0]
    nt = SUBLANES * (k - 1)
    ext = jnp.concatenate([_wrap_tail(prev_tail, x[n - nt:]), x], axis=0)
    out = b + w[k - 1:k] * x
    for j in range(1, k):
        start = nt - SUBLANES * j
        out = out + w[k - 1 - j:k - j] * ext[start:start + n]
    return out


def _causal_mask(n):
    seg = n // SUBLANES
    row = lax.broadcasted_iota(jnp.int32, (n, n), 0)
    col = lax.broadcasted_iota(jnp.int32, (n, n), 1)
    t_row = (row & (SUBLANES - 1)) * seg + (row >> 3)
    t_col = (col & (SUBLANES - 1)) * seg + (col >> 3)
    return t_row >= t_col


def _time_cumsum(tri_bf, steps):
    c = steps.shape[1]
    hi = steps.astype(BF16)
    rest = steps - hi.astype(F32)
    mid = rest.astype(BF16)
    lo = (rest - mid.astype(F32)).astype(BF16)
    top = jnp.dot(tri_bf, jnp.concatenate([hi, mid], axis=1), preferred_element_type=F32)
    return top[:, :c] + (top[:, c:] + jnp.dot(tri_bf, lo, preferred_element_type=F32))


def _lane_pair(col_lo, col_hi, n, width):
    lane = lax.broadcasted_iota(jnp.int32, (n, width), 1)
    return jnp.where(lane < width // 2, col_lo, col_hi)


def _mlstm_project(xc, xm_bf, wqk_ref, wv_ref, h, *, n, dh):
    sl = slice(h * dh, (h + 1) * dh)
    qk = jnp.dot(xc[:, sl].astype(BF16), wqk_ref[h], preferred_element_type=F32)
    v = jnp.dot(xm_bf[:, sl], wv_ref[h], preferred_element_type=F32).astype(BF16)
    return qk[:, :dh].astype(BF16), qk[:, dh:] * (dh ** -0.5), jnp.concatenate([v, jnp.ones((n, dh), BF16)], axis=1)


def _mlstm_decay(g, cs, cs_t, g_t, causal, m_prev, h, *, n):
    bcol = cs[:, GATE_F + h:GATE_F + h + 1]
    brow = cs_t[GATE_F + h:GATE_F + h + 1, :]
    icol = g[:, GATE_I + h:GATE_I + h + 1]
    irow = g_t[GATE_I + h:GATE_I + h + 1, :]
    btot = cs[n - 1:n, GATE_F + h:GATE_F + h + 1]
    d = jnp.where(causal, bcol - brow + irow, -jnp.inf)
    inter = bcol + m_prev
    m_t = jnp.maximum(inter, jnp.max(d, axis=1, keepdims=True))
    wst = btot - bcol + icol
    m_loc = jnp.max(wst, axis=0, keepdims=True)
    m_new = jnp.maximum(btot + m_prev, m_loc)
    return dict(pmat=jnp.exp(d - m_t), e_inter=jnp.exp(inter - m_t), floor=jnp.exp(-m_t),
                ew=jnp.exp(wst - m_loc), s_prev=jnp.exp(btot + m_prev - m_new),
                s_loc=jnp.exp(m_loc - m_new), m_new=m_new)


def _mlstm_output(q, k, v_aug, dec, c_prev, *, dh):
    scores = lax.dot_general(q, k.astype(BF16), (((1,), (1,)), ((), ())),
                             preferred_element_type=F32) * dec["pmat"]
    comb = (jnp.dot(scores.astype(BF16), v_aug, preferred_element_type=F32)
            + dec["e_inter"] * jnp.dot(q, c_prev.astype(BF16), preferred_element_type=F32))
    hh = comb[:, :dh] / jnp.maximum(jnp.abs(comb[:, dh:]), dec["floor"])
    ek = (dec["ew"] * k).astype(BF16)
    c_loc = lax.dot_general(ek, v_aug, (((0,), (0,)), ((), ())), preferred_element_type=F32)
    return hh, dec["s_prev"] * c_prev + dec["s_loc"] * c_loc


def _rglru_scan(a, u, carry):
    n, d = a.shape
    tiles = n // SUBLANES
    hs, ps = [u[0:SUBLANES]], [a[0:SUBLANES]]
    for i in range(1, tiles):
        rs = slice(i * SUBLANES, (i + 1) * SUBLANES)
        hs.append(a[rs] * hs[-1] + u[rs])
        ps.append(a[rs] * ps[-1])
    gacc, pacc = hs[-1], ps[-1]
    sub = lax.broadcasted_iota(jnp.int32, (SUBLANES, d), 0)
    for s in (1, 2, 4):
        keep = sub >= s
        gacc = jnp.where(keep, pacc * pltpu.roll(gacc, s, 0) + gacc, gacc)
        pacc = jnp.where(keep, pacc * pltpu.roll(pacc, s, 0), pacc)
    seg_end = gacc + pacc * carry
    seg_in = jnp.where(sub == 0, carry, pltpu.roll(seg_end, 1, 0))
    out = [hs[i] + ps[i] * seg_in for i in range(tiles)]
    return jnp.concatenate(out, axis=0), seg_end[SUBLANES - 1:SUBLANES]


def _ssd_decay(dt, acs, acs_t, causal, h0, *, n, hd):
    pair = 2 * hd
    cols = [GATE_DT + h0, GATE_DT + h0 + 1]
    acol = [acs[:, c:c + 1] for c in cols]
    arow = [acs_t[c:c + 1, :] for c in cols]
    atot = [acs[n - 1:n, c:c + 1] for c in cols]
    return dict(
        dec=[jnp.exp(jnp.where(causal, acol[e] - arow[e], -jnp.inf)) for e in range(2)],
        dt=_lane_pair(dt[:, cols[0]:cols[0] + 1], dt[:, cols[1]:cols[1] + 1], n, pair),
        e_in=jnp.exp(_lane_pair(acol[0], acol[1], n, pair)),
        e_out=jnp.exp(_lane_pair(atot[0] - acol[0], atot[1] - acol[1], n, pair)),
        e_tot=jnp.exp(_lane_pair(atot[0], atot[1], 1, pair)))


def _ssd_output(xs_p, cbm, dec, yoff_p, dskip_p, *, n, hd):
    pair = 2 * hd
    lo = lax.broadcasted_iota(jnp.int32, (n, pair), 1) < hd
    xdt = xs_p * dec["dt"]
    ydiag = jnp.zeros((n, pair), F32)
    for e in range(2):
        half = jnp.where(lo if e == 0 else jnp.logical_not(lo), xdt, 0.0).astype(BF16)
        ydiag = ydiag + jnp.dot((cbm * dec["dec"][e]).astype(BF16), half, preferred_element_type=F32)
    return ydiag + yoff_p * dec["e_in"] + xs_p * dskip_p, (xdt * dec["e_out"]).astype(BF16)


def _front_kernel(xn_ref, xres_ref, gin_ref, win_ref, cw_ref, cb_ref, wqk_ref, wv_ref, gb_ref, gnm_ref,
                  wa_ref, wx_ref, ba_ref, bx_ref, lam_ref, alog_ref, dskip_ref, gns_ref,
                  wo_ref, gpost_ref, gffn_ref, x1_ref, h2_ref,
                  pbuf, ybuf, tail_ref, c_ref, m_ref, h_ref, s_ref, *, n, steps_per_seq, dm, dr, ds, n_main):
    step = pl.program_id(0)

    def reset_state():
        tail_ref[...] = jnp.zeros_like(tail_ref)
        c_ref[...] = jnp.zeros_like(c_ref)
        m_ref[...] = jnp.zeros_like(m_ref)
        h_ref[...] = jnp.zeros_like(h_ref)
        s_ref[...] = jnp.zeros_like(s_ref)

    @pl.when(step == 0)
    def _first():
        pbuf[1] = jnp.zeros(pbuf.shape[1:], pbuf.dtype)
        ybuf[...] = jnp.zeros_like(ybuf)
        reset_state()

    for half in range(2):
        if half == 1:
            pl.when(lax.rem(2 * step, steps_per_seq) == 0)(reset_state)
        blk = slice(half * n, (half + 1) * n)
        _front_block(xn_ref.at[blk], xres_ref, gin_ref, win_ref, cw_ref, cb_ref, wqk_ref, wv_ref, gb_ref,
                     gnm_ref, wa_ref, wx_ref, ba_ref, bx_ref, lam_ref, alog_ref, dskip_ref, gns_ref, wo_ref,
                     gpost_ref, gffn_ref, x1_ref, h2_ref, pbuf.at[half], pbuf.at[1 - half], ybuf,
                     ybuf.at[slice((1 - half) * n, (2 - half) * n)],
                     tail_ref, c_ref, m_ref, h_ref, s_ref, n=n, dm=dm, dr=dr, ds=ds, n_main=n_main,
                     out_project=(half == 1))


def _front_block(xn_ref, xres_ref, gin_ref, win_ref, cw_ref, cb_ref, wqk_ref, wv_ref, gb_ref, gnm_ref,
                 wa_ref, wx_ref, ba_ref, bx_ref, lam_ref, alog_ref, dskip_ref, gns_ref,
                 wo_ref, gpost_ref, gffn_ref, x1_ref, h2_ref, p_out, p_ref,
                 y_all, ybuf, tail_ref, c_ref, m_ref, h_ref, s_ref, *, n, dm, dr, ds, n_main, out_project):
    heads_m = MLSTM_HEADS
    dh = dm // heads_m
    hd = SSD_HEAD_DIM
    pair = 2 * hd
    n_pairs = ds // pair
    pairs_per_group = n_pairs // SSD_GROUPS
    n_all = win_ref.shape[1]
    n_conv = dm + dr + 2 * ds
    o0 = n_conv
    ns = SSD_STATE

    if out_project:
        mix = jnp.dot(y_all[...], wo_ref[...], preferred_element_type=F32)

    h_next = _rms(xn_ref[...], gin_ref[...]).astype(BF16)
    starts = list(range(0, n_all, FRONT_COLS))
    chunks = n // PERM
    per_call = -(-len(starts) // (8 * chunks))

    def project(k=per_call):
        for _ in range(k):
            if starts:
                c0 = starts.pop(0)
                c1 = min(c0 + FRONT_COLS, n_all)
                p_out[:, c0:c1] = jnp.dot(h_next, win_ref[:, c0:c1], preferred_element_type=F32)

    causal = _causal_mask(PERM)
    tri = jnp.where(causal, 1.0, 0.0).astype(BF16)
    lane = lax.broadcasted_iota(jnp.int32, (1, LANES), 1)

    for ch in range(chunks):
        rows = slice(ch * PERM, (ch + 1) * PERM)

        g = p_ref[rows, n_main:] + gb_ref[...]
        sp = _softplus(jnp.where(lane < GATE_DT, -g, g))
        dt = sp
        steps = jnp.where(lane < GATE_DT, -sp, sp * (-jnp.exp(alog_ref[...])))
        cs = _time_cumsum(tri, steps)
        project()

        nt_rows = tail_ref.shape[0]
        x_in = p_ref[rows, :n_conv]
        conv = _perm_conv(x_in, tail_ref[...], cw_ref[...], cb_ref[...])
        tail_ref[...] = x_in[PERM - nt_rows:]
        project()
        xc_m = _silu(conv[:, :dm])
        xc_r = conv[:, dm:dm + dr]
        xs_bc = _silu(conv[:, dm + dr:])
        cs_t, g_t = cs.T, g.T

        if ch == 0 and out_project:
            x1 = xres_ref[...] + _rms(mix, gpost_ref[...])
            x1_ref[...] = x1
            h2_ref[...] = _rms(x1, gffn_ref[...]).astype(h2_ref.dtype)
        project()

        xm_bf = x_in[:, :dm].astype(BF16)
        qkv = [_mlstm_project(xc_m, xm_bf, wqk_ref, wv_ref, h, n=PERM, dh=dh) for h in range(heads_m)]
        xr_bf = xc_r.astype(BF16)
        tile = wa_ref.shape[1]
        ra = [jnp.dot(xr_bf[:, j * tile:(j + 1) * tile], wa_ref[j], preferred_element_type=F32)
              for j in range(dr // tile)]
        ri = [jnp.dot(xr_bf[:, j * tile:(j + 1) * tile], wx_ref[j], preferred_element_type=F32)
              for j in range(dr // tile)]
        b_g = [xs_bc[:, ds + grp * ns:ds + (grp + 1) * ns].astype(BF16) for grp in range(SSD_GROUPS)]
        c_g = [xs_bc[:, ds + (SSD_GROUPS + grp) * ns:ds + (SSD_GROUPS + grp + 1) * ns].astype(BF16)
               for grp in range(SSD_GROUPS)]
        cbm = [lax.dot_general(c_g[grp], b_g[grp], (((1,), (1,)), ((), ())), preferred_element_type=F32)
               for grp in range(SSD_GROUPS)]

        r = _sigmoid(jnp.concatenate(ra, axis=1) + ba_ref[...])
        i = _sigmoid(jnp.concatenate(ri, axis=1) + bx_ref[...])
        a = jnp.exp((-RGLRU_C) * r * _softplus(-lam_ref[...]))
        w = 1.0 - a * a
        u = jnp.where(w > 0.0, w * lax.rsqrt(w), 0.0) * (i * xc_r)
        project()
        hseq, h_ref[...] = _rglru_scan(a, u, h_ref[...])
        yr = p_ref[rows, o0 + dm:o0 + dm + dr]
        ybuf[rows, dm:dm + dr] = (hseq * jax.nn.gelu(yr, approximate=True)).astype(ybuf.dtype)
        project()

        dec_m = [_mlstm_decay(g, cs, cs_t, g_t, causal, m_ref[h:h + 1, 0:1], h, n=PERM) for h in range(heads_m)]
        dec_s = [_ssd_decay(dt, cs, cs_t, causal, 2 * p, n=PERM, hd=hd) for p in range(n_pairs)]
        gw = pairs_per_group * pair
        yoff = [jnp.dot(c_g[grp], s_ref[:, grp * gw:(grp + 1) * gw].astype(BF16), preferred_element_type=F32)
                for grp in range(SSD_GROUPS)]
        ys, xds = [], []
        for j in range(max(heads_m, n_pairs)):
            if j < heads_m:
                sl = slice(j * dh, (j + 1) * dh)
                hh, c_ref[j] = _mlstm_output(*qkv[j], dec_m[j], c_ref[j], dh=dh)
                m_ref[j:j + 1, :] = jnp.broadcast_to(dec_m[j]["m_new"], (1, LANES))
                o = _sigmoid(p_ref[rows, o0 + j * dh:o0 + (j + 1) * dh]) * hh
                ybuf[rows, sl] = _rms(o, gnm_ref[:, sl]).astype(ybuf.dtype)
            if j % 2 == 1:
                project()
            if j < n_pairs:
                ps = slice(j * pair, (j + 1) * pair)
                grp, jg = divmod(j, pairs_per_group)
                y_p, xd_p = _ssd_output(xs_bc[:, ps], cbm[grp], dec_s[j], yoff[grp][:, jg * pair:(jg + 1) * pair],
                                        dskip_ref[:, ps], n=PERM, hd=hd)
                ys.append(y_p)
                xds.append(xd_p)
                if jg == pairs_per_group - 1:
                    gs = slice(grp * gw, (grp + 1) * gw)
                    s_loc = lax.dot_general(b_g[grp], jnp.concatenate(xds[-pairs_per_group:], axis=1),
                                            (((0,), (0,)), ((), ())), preferred_element_type=F32)
                    e_tot = jnp.concatenate([dec_s[p]["e_tot"] for p in range(j + 1 - pairs_per_group, j + 1)],
                                            axis=1)
                    s_ref[:, gs] = e_tot * s_ref[:, gs] + s_loc
        project()
        z = p_ref[rows, o0 + dm + dr:o0 + dm + dr + ds]
        ybuf[rows, dm + dr:] = _rms(jnp.concatenate(ys, axis=1) * _silu(z), gns_ref[...]).astype(ybuf.dtype)
    project(len(starts))


def _front(x2, gin, win, cw, cb, wqk, wv, gb, gnm, wa, wx, ba, bx, lam, alog, dskip, gns, wo, gpost, gffn,
           *, batch, seq, n_main):
    n = ROWS_FRONT
    t, d = x2.shape
    dm, dr, ds = gnm.shape[1], lam.shape[1], gns.shape[1]
    nt = seq // n
    nblk = batch * nt
    k = cw.shape[0]
    consts = [gin, win, cw, cb, wqk, wv, gb, gnm, wa, wx, ba, bx, lam, alog, dskip, gns, wo, gpost, gffn]
    const_spec = lambda a: pl.BlockSpec(a.shape, lambda s, _nd=a.ndim: (0,) * _nd)
    nstep = nblk // 2
    nxt = lambda s: (jnp.minimum(s, nstep - 1), 0)
    done = lambda s: (jnp.clip(s - 1, 0, nstep - 1), 0)
    return pl.pallas_call(
        functools.partial(_front_kernel, n=n, steps_per_seq=nt, dm=dm, dr=dr, ds=ds, n_main=n_main),
        grid=(nstep + 1,),
        in_specs=[pl.BlockSpec((2 * n, d), nxt), pl.BlockSpec((2 * n, d), done)] + [const_spec(a) for a in consts],
        out_specs=[pl.BlockSpec((2 * n, d), done), pl.BlockSpec((2 * n, d), done)],
        out_shape=[jax.ShapeDtypeStruct((t, d), F32), jax.ShapeDtypeStruct((t, d), BF16)],
        scratch_shapes=[
            pltpu.VMEM((2, n, win.shape[1]), F32),
            pltpu.VMEM((2 * n, dm + dr + ds), BF16),
            pltpu.VMEM((SUBLANES * (k - 1), cw.shape[1]), F32),
            pltpu.VMEM((MLSTM_HEADS, dm // MLSTM_HEADS, 2 * dm // MLSTM_HEADS), F32),
            pltpu.VMEM((SUBLANES, LANES), F32),
            pltpu.VMEM((1, dr), F32),
            pltpu.VMEM((SSD_STATE, ds), F32),
        ],
        compiler_params=pltpu.CompilerParams(
            dimension_semantics=("arbitrary",), vmem_limit_bytes=VMEM_LIMIT),
        name="front",
    )(x2, x2, *consts)


def _ffn_kernel(h_ref, hh_ref, x_ref, wup_ref, cw_ref, cb_ref, wdn_ref, gpost_ref, o_ref, he_ref, acc_ref,
                *natural, d_ff, cols, sub):
    t = pl.program_id(1)
    tm = h_ref.shape[0]
    halo = FFN_HALO
    step = pl.program_id(0) * pl.num_programs(1) + t
    last_step = pl.num_programs(0) * pl.num_programs(1) - 1

    def out_copies(r0):
        stage_ref, sems = natural
        copies = []
        for p0 in range(r0, r0 + sub, PERM):
            blk = (step * tm + p0) // PERM
            for i in range(PERM // SUBLANES):
                copies.append(pltpu.make_async_copy(stage_ref.at[pl.ds(p0 + i * SUBLANES, SUBLANES), :],
                                                    o_ref.at[blk, :, i, :], sems.at[r0 // sub]))
        return copies
    he_ref[0:halo, :] = jnp.where(t > 0, hh_ref[...], jnp.zeros_like(hh_ref))
    he_ref[halo:, :] = h_ref[...]

    def up(item):
        r0, c0 = item
        he = he_ref[r0:r0 + halo + sub, :]
        return [jnp.dot(he, wup_ref[:, base:base + cols], preferred_element_type=F32)
                for base in (c0, d_ff + c0)]

    def gate(item, us):
        _, c0 = item
        branch = []
        for base, u in zip((c0, d_ff + c0), us):
            cs = slice(base, base + cols)
            w = cw_ref[:, cs]
            b = cb_ref[:, cs]
            parts = []
            for p0 in range(0, sub, PERM):
                blk = u[halo + p0:halo + p0 + PERM]
                parts.append(_perm_conv(blk, u[p0:p0 + halo], w, b))
            branch.append(jnp.concatenate(parts, axis=0))
        return (jax.nn.gelu(branch[0], approximate=True) * branch[1]).astype(BF16)

    items = [(r0, c0) for r0 in range(0, tm, sub) for c0 in range(0, d_ff, cols)]
    us = up(items[0])
    for i, item in enumerate(items):
        us_next = up(items[i + 1]) if i + 1 < len(items) else None
        r0, c0 = item
        rs = slice(r0, r0 + sub)
        down = jnp.dot(gate(item, us), wdn_ref[c0:c0 + cols, :], preferred_element_type=F32)
        if c0 == 0:
            acc_ref[rs, :] = down
        else:
            acc_ref[rs, :] += down
        if c0 + cols >= d_ff:
            o = x_ref[rs, :] + _rms(acc_ref[rs, :], gpost_ref[...])
            if natural:
                @pl.when(step > 0)
                def _drain():
                    for c in out_copies(r0):
                        c.wait()
                natural[0][rs, :] = o
                for c in out_copies(r0):
                    c.start()
            else:
                o_ref[rs, :] = o
        us = us_next
    if natural:
        @pl.when(step == last_step)
        def _finish():
            for r0 in range(0, tm, sub):
                for c in out_copies(r0):
                    c.wait()


def _ffn(h2, x1, wup, cw, cb, wdn, gpost, *, batch, seq, natural_out):
    t, d = x1.shape
    d_ff = wdn.shape[0]
    tm = ROWS_FFN
    nt = seq // tm
    assert FFN_HALO == SUBLANES * (cw.shape[0] - 1)
    rows = lambda b, i: (b * nt + i, 0)
    halo = lambda b, i: (jnp.maximum((b * seq + i * tm) // FFN_HALO - 1, 0), 0)
    const = lambda b, i: (0, 0)
    return pl.pallas_call(
        functools.partial(_ffn_kernel, d_ff=d_ff, cols=FFN_COLS, sub=FFN_SUB),
        grid=(batch, nt),
        in_specs=[
            pl.BlockSpec((tm, d), rows),
            pl.BlockSpec((FFN_HALO, d), halo),
            pl.BlockSpec((tm, d), rows),
            pl.BlockSpec(wup.shape, const),
            pl.BlockSpec(cw.shape, const),
            pl.BlockSpec(cb.shape, const),
            pl.BlockSpec(wdn.shape, const),
            pl.BlockSpec((1, d), const),
        ],
        out_specs=pl.BlockSpec(memory_space=pl.ANY) if natural_out else pl.BlockSpec((tm, d), rows),
        out_shape=jax.ShapeDtypeStruct((t // PERM, SUBLANES, PERM // SUBLANES, d) if natural_out else (t, d), F32),
        scratch_shapes=[pltpu.VMEM((FFN_HALO + tm, d), BF16), pltpu.VMEM((tm, d), F32)]
        + ([pltpu.VMEM((tm, d), F32), pltpu.SemaphoreType.DMA((tm // FFN_SUB,))] if natural_out else []),
        compiler_params=pltpu.CompilerParams(
            dimension_semantics=("arbitrary", "arbitrary"), vmem_limit_bytes=VMEM_LIMIT),
        name="ffn",
    )(h2, h2, x1, wup, cw, cb, wdn, gpost)


def _cast_kernel(w_ref, o_ref):
    o_ref[...] = w_ref[...].astype(o_ref.dtype)


def _layer_bf16(w, layer):
    _, r, c = w.shape
    tr = ROWS_CAST
    return pl.pallas_call(
        _cast_kernel,
        grid=(r // tr,),
        in_specs=[pl.BlockSpec((None, tr, c), lambda i: (layer, i, 0))],
        out_specs=pl.BlockSpec((tr, c), lambda i: (i, 0)),
        out_shape=jax.ShapeDtypeStruct((r, c), BF16),
        compiler_params=pltpu.CompilerParams(
            dimension_semantics=("arbitrary",), vmem_limit_bytes=VMEM_LIMIT),
        name="cast",
    )(w)


def _block_diag(w, per_tile):
    nb, e, _ = w.shape
    w = w.reshape(nb // per_tile, per_tile, e, e)
    eye = jnp.eye(per_tile, dtype=w.dtype)
    out = jnp.einsum("tpij,pq->tpiqj", w, eye)
    return out.reshape(nb // per_tile, per_tile * e, per_tile * e)


def _to_segment_order(x):
    b, s, d = x.shape
    x = x.reshape(b, s // PERM, SUBLANES, PERM // SUBLANES, d)
    return jnp.swapaxes(x, 2, 3).reshape(b * s, d)


def kernel(x, norm_mix_pre, norm_mix_post, norm_ffn_pre, norm_ffn_post, w_in, conv_m_w, conv_m_b, w_q_m, w_k_m, w_v_m, b_i_m, b_f_m, norm_m, conv_r_w, conv_r_b, w_a_r, b_a_r, w_x_r, b_x_r, lam_r, conv_s_w, conv_s_b, dt_bias_s, a_log_s, d_skip_s, norm_s, w_out, w_up, conv_f_w, conv_f_b, w_down):
    batch, seq, d = x.shape
    depth = w_in.shape[0]
    dm = conv_m_w.shape[2]
    dr = conv_r_w.shape[2]
    ds = norm_s.shape[1]
    dconv = conv_s_w.shape[2]
    hm = b_i_m.shape[1]
    hs = dt_bias_s.shape[1]
    sizes = (dm, dm, hm, hm, dr, dr, ds, dconv, hs)
    offs = [0]
    for s in sizes:
        offs.append(offs[-1] + s)
    col = lambda j: (offs[j], offs[j + 1])
    main_cols = [col(0), col(4), col(7), col(1), col(5), col(6)]
    n_main = sum(b - a for a, b in main_cols)

    row = lambda v: v.reshape(1, -1).astype(F32)
    x2 = _to_segment_order(x)
    w_in_bf = w_in.astype(BF16)
    for l in range(depth):
        wl = w_in_bf[l]
        gate_w = jnp.concatenate(
            [wl[:, offs[2]:offs[4]], wl[:, offs[8]:offs[9]],
             jnp.zeros((d, LANES - 2 * hm - hs), BF16)], axis=1)
        w_all = jnp.concatenate([wl[:, a:b] for a, b in main_cols] + [gate_w], axis=1)
        gate_b = jnp.concatenate([b_i_m[l], b_f_m[l], dt_bias_s[l],
                                  jnp.zeros((LANES - 2 * hm - hs,), F32)]).reshape(1, LANES)
        alog = jnp.concatenate([jnp.zeros((GATE_DT,), F32), a_log_s[l],
                                jnp.zeros((LANES - GATE_DT - hs,), F32)]).reshape(1, LANES)
        cw = jnp.concatenate([conv_m_w[l], conv_r_w[l], conv_s_w[l]], axis=1)
        cb = jnp.concatenate([conv_m_b[l], conv_r_b[l], conv_s_b[l]]).reshape(1, -1)

        x1, h2 = _front(x2, row(norm_mix_pre[l]), w_all, cw, cb, jnp.concatenate([w_q_m[l], w_k_m[l]], axis=-1).astype(BF16),
                        w_v_m[l].astype(BF16), gate_b, row(norm_m[l]), _block_diag(w_a_r[l], 4).astype(BF16),
                        _block_diag(w_x_r[l], 4).astype(BF16), row(b_a_r[l]), row(b_x_r[l]), row(lam_r[l]),
                        alog, row(jnp.repeat(d_skip_s[l], SSD_HEAD_DIM)), row(norm_s[l]),
                        _layer_bf16(w_out, l), row(norm_mix_post[l]), row(norm_ffn_pre[l]),
                        batch=batch, seq=seq, n_main=n_main)
        x2 = _ffn(h2, x1, _layer_bf16(w_up, l), conv_f_w[l], row(conv_f_b[l]),
                  _layer_bf16(w_down, l), row(norm_ffn_post[l]), batch=batch, seq=seq,
                  natural_out=(l == depth - 1))
    return x2.reshape(batch, seq, d)
```

```python
import functools

import jax
import jax.numpy as jnp
from jax import lax
from jax.experimental import pallas as pl
from jax.experimental.pallas import tpu as pltpu

F32 = jnp.float32
BF16 = jnp.bfloat16
EPS = 1e-6

LANES = 128
SUBLANES = 8
VMEM_LIMIT = 56 * 1024 * 1024

MLSTM_HEADS = 4
RGLRU_C = 8.0
SSD_HEAD_DIM = 64
SSD_GROUPS = 2
SSD_STATE = 128
GATE_I, GATE_F, GATE_DT = 0, 4, 8

PERM = 128
ROWS_FRONT = 256
FRONT_COLS = 256
ROWS_FFN = 1024
FFN_SUB = 256
ROWS_CAST = 256
FFN_COLS = 256
FFN_HALO = 16


def _rms(x, g):
    return x * lax.rsqrt(jnp.mean(x * x, axis=-1, keepdims=True) + EPS) * g


def _softplus(x):
    return jnp.maximum(x, 0.0) + jnp.log(1.0 + jnp.exp(-jnp.abs(x)))


def _sigmoid(x):
    return 0.5 * jnp.tanh(0.5 * x) + 0.5


def _silu(x):
    hx = 0.5 * x
    return hx * jnp.tanh(hx) + hx


def _wrap_tail(prev_tail, tail):
    out = []
    for j in range(tail.shape[0] // SUBLANES):
        rs = slice(j * SUBLANES, (j + 1) * SUBLANES)
        sub = lax.broadcasted_iota(jnp.int32, (SUBLANES, tail.shape[1]), 0)
        mixed = jnp.where(sub == SUBLANES - 1, prev_tail[rs], tail[rs])
        out.append(pltpu.roll(mixed, 1, 0))
    return jnp.concatenate(out, axis=0)


def _perm_conv(x, prev_tail, w, b):
    n = x.shape[0]
    k = w.shape[0]
    nt = SUBLANES * (k - 1)
    ext = jnp.concatenate([_wrap_tail(prev_tail, x[n - nt:]), x], axis=0)
    out = b + w[k - 1:k] * x
    for j in range(1, k):
        start = nt - SUBLANES * j
        out = out + w[k - 1 - j:k - j] * ext[start:start + n]
    return out


def _causal_mask(n):
    seg = n // SUBLANES
    row = lax.broadcasted_iota(jnp.int32, (n, n), 0)
    col = lax.broadcasted_iota(jnp.int32, (n, n), 1)
    t_row = (row & (SUBLANES - 1)) * seg + (row >> 3)
    t_col = (col & (SUBLANES - 1)) * seg + (col >> 3)
    return t_row >= t_col


def _time_cumsum(tri_bf, steps):
    c = steps.shape[1]
    hi = steps.astype(BF16)
    rest = steps - hi.astype(F32)
    mid = rest.astype(BF16)
    lo = (rest - mid.astype(F32)).astype(BF16)
    top = jnp.dot(tri_bf, jnp.concatenate([hi, mid], axis=1), preferred_element_type=F32)
    return top[:, :c] + (top[:, c:] + jnp.dot(tri_bf, lo, preferred_element_type=F32))


def _lane_pair(col_lo, col_hi, n, width):
    lane = lax.broadcasted_iota(jnp.int32, (n, width), 1)
    return jnp.where(lane < width // 2, col_lo, col_hi)


def _mlstm_project(xc, xm_bf, wqk_ref, wv_ref, h, *, n, dh):
    sl = slice(h * dh, (h + 1) * dh)
    qk = jnp.dot(xc[:, sl].astype(BF16), wqk_ref[h], preferred_element_type=F32)
    v = jnp.dot(xm_bf[:, sl], wv_ref[h], preferred_element_type=F32).astype(BF16)
    return qk[:, :dh].astype(BF16), qk[:, dh:] * (dh ** -0.5), jnp.concatenate([v, jnp.ones((n, dh), BF16)], axis=1)


def _mlstm_decay(g, cs, cs_t, g_t, causal, m_prev, h, *, n):
    bcol = cs[:, GATE_F + h:GATE_F + h + 1]
    brow = cs_t[GATE_F + h:GATE_F + h + 1, :]
    icol = g[:, GATE_I + h:GATE_I + h + 1]
    irow = g_t[GATE_I + h:GATE_I + h + 1, :]
    btot = cs[n - 1:n, GATE_F + h:GATE_F + h + 1]
    d = jnp.where(causal, bcol - brow + irow, -jnp.inf)
    inter = bcol + m_prev
    m_t = jnp.maximum(inter, jnp.max(d, axis=1, keepdims=True))
    wst = btot - bcol + icol
    m_loc = jnp.max(wst, axis=0, keepdims=True)
    m_new = jnp.maximum(btot + m_prev, m_loc)
    return dict(pmat=jnp.exp(d - m_t), e_inter=jnp.exp(inter - m_t), floor=jnp.exp(-m_t),
                ew=jnp.exp(wst - m_loc), s_prev=jnp.exp(btot + m_prev - m_new),
                s_loc=jnp.exp(m_loc - m_new), m_new=m_new)


def _mlstm_output(q, k, v_aug, dec, c_prev, *, dh):
    scores = lax.dot_general(q, k.astype(BF16), (((1,), (1,)), ((), ())),
                             preferred_element_type=F32) * dec["pmat"]
    comb = (jnp.dot(scores.astype(BF16), v_aug, preferred_element_type=F32)
            + dec["e_inter"] * jnp.dot(q, c_prev.astype(BF16), preferred_element_type=F32))
    hh = comb[:, :dh] / jnp.maximum(jnp.abs(comb[:, dh:]), dec["floor"])
    ek = (dec["ew"] * k).astype(BF16)
    c_loc = lax.dot_general(ek, v_aug, (((0,), (0,)), ((), ())), preferred_element_type=F32)
    return hh, dec["s_prev"] * c_prev + dec["s_loc"] * c_loc


def _rglru_scan(a, u, carry):
    n, d = a.shape
    tiles = n // SUBLANES
    hs, ps = [u[0:SUBLANES]], [a[0:SUBLANES]]
    for i in range(1, tiles):
        rs = slice(i * SUBLANES, (i + 1) * SUBLANES)
        hs.append(a[rs] * hs[-1] + u[rs])
        ps.append(a[rs] * ps[-1])
    gacc, pacc = hs[-1], ps[-1]
    sub = lax.broadcasted_iota(jnp.int32, (SUBLANES, d), 0)
    for s in (1, 2, 4):
        keep = sub >= s
        gacc = jnp.where(keep, pacc * pltpu.roll(gacc, s, 0) + gacc, gacc)
        pacc = jnp.where(keep, pacc * pltpu.roll(pacc, s, 0), pacc)
    seg_end = gacc + pacc * carry
    seg_in = jnp.where(sub == 0, carry, pltpu.roll(seg_end, 1, 0))
    out = [hs[i] + ps[i] * seg_in for i in range(tiles)]
    return jnp.concatenate(out, axis=0), seg_end[SUBLANES - 1:SUBLANES]


def _ssd_decay(dt, acs, acs_t, causal, h0, *, n, hd):
    pair = 2 * hd
    cols = [GATE_DT + h0, GATE_DT + h0 + 1]
    acol = [acs[:, c:c + 1] for c in cols]
    arow = [acs_t[c:c + 1, :] for c in cols]
    atot = [acs[n - 1:n, c:c + 1] for c in cols]
    return dict(
        dec=[jnp.exp(jnp.where(causal, acol[e] - arow[e], -jnp.inf)) for e in range(2)],
        dt=_lane_pair(dt[:, cols[0]:cols[0] + 1], dt[:, cols[1]:cols[1] + 1], n, pair),
        e_in=jnp.exp(_lane_pair(acol[0], acol[1], n, pair)),
        e_out=jnp.exp(_lane_pair(atot[0] - acol[0], atot[1] - acol[1], n, pair)),
        e_tot=jnp.exp(_lane_pair(atot[0], atot[1], 1, pair)))


def _ssd_output(xs_p, cbm, dec, yoff_p, dskip_p, *, n, hd):
    pair = 2 * hd
    lo = lax.broadcasted_iota(jnp.int32, (n, pair), 1) < hd
    xdt = xs_p * dec["dt"]
    ydiag = jnp.zeros((n, pair), F32)
    for e in range(2):
        half = jnp.where(lo if e == 0 else jnp.logical_not(lo), xdt, 0.0).astype(BF16)
        ydiag = ydiag + jnp.dot((cbm * dec["dec"][e]).astype(BF16), half, preferred_element_type=F32)
    return ydiag + yoff_p * dec["e_in"] + xs_p * dskip_p, (xdt * dec["e_out"]).astype(BF16)


def _front_kernel(xn_ref, xres_ref, gin_ref, win_ref, cw_ref, cb_ref, wqk_ref, wv_ref, gb_ref, gnm_ref,
                  wa_ref, wx_ref, ba_ref, bx_ref, lam_ref, alog_ref, dskip_ref, gns_ref,
                  wo_ref, gpost_ref, gffn_ref, x1_ref, h2_ref,
                  pbuf, ybuf, tail_ref, c_ref, m_ref, h_ref, s_ref, *, n, steps_per_seq, dm, dr, ds, n_main):
    step = pl.program_id(0)

    def reset_state():
        tail_ref[...] = jnp.zeros_like(tail_ref)
        c_ref[...] = jnp.zeros_like(c_ref)
        m_ref[...] = jnp.zeros_like(m_ref)
        h_ref[...] = jnp.zeros_like(h_ref)
        s_ref[...] = jnp.zeros_like(s_ref)

    @pl.when(step == 0)
    def _first():
        pbuf[1] = jnp.zeros(pbuf.shape[1:], pbuf.dtype)
        ybuf[...] = jnp.zeros_like(ybuf)
        reset_state()

    for half in range(2):
        if half == 1:
            pl.when(lax.rem(2 * step, steps_per_seq) == 0)(reset_state)
        blk = slice(half * n, (half + 1) * n)
        _front_block(xn_ref.at[blk], xres_ref, gin_ref, win_ref, cw_ref, cb_ref, wqk_ref, wv_ref, gb_ref,
                     gnm_ref, wa_ref, wx_ref, ba_ref, bx_ref, lam_ref, alog_ref, dskip_ref, gns_ref, wo_ref,
                     gpost_ref, gffn_ref, x1_ref, h2_ref, pbuf.at[half], pbuf.at[1 - half], ybuf,
                     ybuf.at[slice((1 - half) * n, (2 - half) * n)],
                     tail_ref, c_ref, m_ref, h_ref, s_ref, n=n, dm=dm, dr=dr, ds=ds, n_main=n_main,
                     out_project=(half == 1))


def _front_block(xn_ref, xres_ref, gin_ref, win_ref, cw_ref, cb_ref, wqk_ref, wv_ref, gb_ref, gnm_ref,
                 wa_ref, wx_ref, ba_ref, bx_ref, lam_ref, alog_ref, dskip_ref, gns_ref,
                 wo_ref, gpost_ref, gffn_ref, x1_ref, h2_ref, p_out, p_ref,
                 y_all, ybuf, tail_ref, c_ref, m_ref, h_ref, s_ref, *, n, dm, dr, ds, n_main, out_project):
    heads_m = MLSTM_HEADS
    dh = dm // heads_m
    hd = SSD_HEAD_DIM
    pair = 2 * hd
    n_pairs = ds // pair
    pairs_per_group = n_pairs // SSD_GROUPS
    n_all = win_ref.shape[1]
    n_conv = dm + dr + 2 * ds
    o0 = n_conv
    ns = SSD_STATE

    if out_project:
        mix = jnp.dot(y_all[...], wo_ref[...], preferred_element_type=F32)

    h_next = _rms(xn_ref[...], gin_ref[...]).astype(BF16)
    starts = list(range(0, n_all, FRONT_COLS))
    chunks = n // PERM
    per_call = -(-len(starts) // (8 * chunks))

    def project(k=per_call):
        for _ in range(k):
            if starts:
                c0 = starts.pop(0)
                c1 = min(c0 + FRONT_COLS, n_all)
                p_out[:, c0:c1] = jnp.dot(h_next, win_ref[:, c0:c1], preferred_element_type=F32)

    causal = _causal_mask(PERM)
    tri = jnp.where(causal, 1.0, 0.0).astype(BF16)
    lane = lax.broadcasted_iota(jnp.int32, (1, LANES), 1)

    for ch in range(chunks):
        rows = slice(ch * PERM, (ch + 1) * PERM)

        g = p_ref[rows, n_main:] + gb_ref[...]
        sp = _softplus(jnp.where(lane < GATE_DT, -g, g))
        dt = sp
        steps = jnp.where(lane < GATE_DT, -sp, sp * (-jnp.exp(alog_ref[...])))
        cs = _time_cumsum(tri, steps)
        project()

        nt_rows = tail_ref.shape[0]
        x_in = p_ref[rows, :n_conv]
        conv = _perm_conv(x_in, tail_ref[...], cw_ref[...], cb_ref[...])
        tail_ref[...] = x_in[PERM - nt_rows:]
        project()
        xc_m = _silu(conv[:, :dm])
        xc_r = conv[:, dm:dm + dr]
        xs_bc = _silu(conv[:, dm + dr:])
        cs_t, g_t = cs.T, g.T

        if ch == 0 and out_project:
            x1 = xres_ref[...] + _rms(mix, gpost_ref[...])
            x1_ref[...] = x1
            h2_ref[...] = _rms(x1, gffn_ref[...]).astype(h2_ref.dtype)
        project()

        xm_bf = x_in[:, :dm].astype(BF16)
        qkv = [_mlstm_project(xc_m, xm_bf, wqk_ref, wv_ref, h, n=PERM, dh=dh) for h in range(heads_m)]
        xr_bf = xc_r.astype(BF16)
        tile = wa_ref.shape[1]
        ra = [jnp.dot(xr_bf[:, j * tile:(j + 1) * tile], wa_ref[j], preferred_element_type=F32)
              for j in range(dr // tile)]
        ri = [jnp.dot(xr_bf[:, j * tile:(j + 1) * tile], wx_ref[j], preferred_element_type=F32)
              for j in range(dr // tile)]
        b_g = [xs_bc[:, ds + grp * ns:ds + (grp + 1) * ns].astype(BF16) for grp in range(SSD_GROUPS)]
        c_g = [xs_bc[:, ds + (SSD_GROUPS + grp) * ns:ds + (SSD_GROUPS + grp + 1) * ns].astype(BF16)
               for grp in range(SSD_GROUPS)]
        cbm = [lax.dot_general(c_g[grp], b_g[grp], (((1,), (1,)), ((), ())), preferred_element_type=F32)
               for grp in range(SSD_GROUPS)]

        r = _sigmoid(jnp.concatenate(ra, axis=1) + ba_ref[...])
        i = _sigmoid(jnp.concatenate(ri, axis=1) + bx_ref[...])
        a = jnp.exp((-RGLRU_C) * r * _softplus(-lam_ref[...]))
        w = 1.0 - a * a
        u = jnp.where(w > 0.0, w * lax.rsqrt(w), 0.0) * (i * xc_r)
        project()
        hseq, h_ref[...] = _rglru_scan(a, u, h_ref[...])
        yr = p_ref[rows, o0 + dm:o0 + dm + dr]
        ybuf[rows, dm:dm + dr] = (hseq * jax.nn.gelu(yr, approximate=True)).astype(ybuf.dtype)
        project()

        dec_m = [_mlstm_decay(g, cs, cs_t, g_t, causal, m_ref[h:h + 1, 0:1], h, n=PERM) for h in range(heads_m)]
        dec_s = [_ssd_decay(dt, cs, cs_t, causal, 2 * p, n=PERM, hd=hd) for p in range(n_pairs)]
        gw = pairs_per_group * pair
        yoff = [jnp.dot(c_g[grp], s_ref[:, grp * gw:(grp + 1) * gw].astype(BF16), preferred_element_type=F32)
                for grp in range(SSD_GROUPS)]
        ys, xds = [], []
        for j in range(max(heads_m, n_pairs)):
            if j < heads_m:
                sl = slice(j * dh, (j + 1) * dh)
                hh, c_ref[j] = _mlstm_output(*qkv[j], dec_m[j], c_ref[j], dh=dh)
                m_ref[j:j + 1, :] = jnp.broadcast_to(dec_m[j]["m_new"], (1, LANES))
                o = _sigmoid(p_ref[rows, o0 + j * dh:o0 + (j + 1) * dh]) * hh
                ybuf[rows, sl] = _rms(o, gnm_ref[:, sl]).astype(ybuf.dtype)
            if j % 2 == 1:
                project()
            if j < n_pairs:
                ps = slice(j * pair, (j + 1) * pair)
                grp, jg = divmod(j, pairs_per_group)
                y_p, xd_p = _ssd_output(xs_bc[:, ps], cbm[grp], dec_s[j], yoff[grp][:, jg * pair:(jg + 1) * pair],
                                        dskip_ref[:, ps], n=PERM, hd=hd)
                ys.append(y_p)
                xds.append(xd_p)
                if jg == pairs_per_group - 1:
                    gs = slice(grp * gw, (grp + 1) * gw)
                    s_loc = lax.dot_general(b_g[grp], jnp.concatenate(xds[-pairs_per_group:], axis=1),
                                            (((0,), (0,)), ((), ())), preferred_element_type=F32)
                    e_tot = jnp.concatenate([dec_s[p]["e_tot"] for p in range(j + 1 - pairs_per_group, j + 1)],
                                            axis=1)
                    s_ref[:, gs] = e_tot * s_ref[:, gs] + s_loc
        project()
        z = p_ref[rows, o0 + dm + dr:o0 + dm + dr + ds]
        ybuf[rows, dm + dr:] = _rms(jnp.concatenate(ys, axis=1) * _silu(z), gns_ref[...]).astype(ybuf.dtype)
    project(len(starts))


def _front(x2, gin, win, cw, cb, wqk, wv, gb, gnm, wa, wx, ba, bx, lam, alog, dskip, gns, wo, gpost, gffn,
           *, batch, seq, n_main):
    n = ROWS_FRONT
    t, d = x2.shape
    dm, dr, ds = gnm.shape[1], lam.shape[1], gns.shape[1]
    nt = seq // n
    nblk = batch * nt
    k = cw.shape[0]
    consts = [gin, win, cw, cb, wqk, wv, gb, gnm, wa, wx, ba, bx, lam, alog, dskip, gns, wo, gpost, gffn]
    const_spec = lambda a: pl.BlockSpec(a.shape, lambda s, _nd=a.ndim: (0,) * _nd)
    nstep = nblk // 2
    nxt = lambda s: (jnp.minimum(s, nstep - 1), 0)
    done = lambda s: (jnp.clip(s - 1, 0, nstep - 1), 0)
    return pl.pallas_call(
        functools.partial(_front_kernel, n=n, steps_per_seq=nt, dm=dm, dr=dr, ds=ds, n_main=n_main),
        grid=(nstep + 1,),
        in_specs=[pl.BlockSpec((2 * n, d), nxt), pl.BlockSpec((2 * n, d), done)] + [const_spec(a) for a in consts],
        out_specs=[pl.BlockSpec((2 * n, d), done), pl.BlockSpec((2 * n, d), done)],
        out_shape=[jax.ShapeDtypeStruct((t, d), F32), jax.ShapeDtypeStruct((t, d), BF16)],
        scratch_shapes=[
            pltpu.VMEM((2, n, win.shape[1]), F32),
            pltpu.VMEM((2 * n, dm + dr + ds), BF16),
            pltpu.VMEM((SUBLANES * (k - 1), cw.shape[1]), F32),
            pltpu.VMEM((MLSTM_HEADS, dm // MLSTM_HEADS, 2 * dm // MLSTM_HEADS), F32),
            pltpu.VMEM((SUBLANES, LANES), F32),
            pltpu.VMEM((1, dr), F32),
            pltpu.VMEM((SSD_STATE, ds), F32),
        ],
        compiler_params=pltpu.CompilerParams(
            dimension_semantics=("arbitrary",), vmem_limit_bytes=VMEM_LIMIT),
        name="front",
    )(x2, x2, *consts)


def _ffn_kernel(h_ref, hh_ref, x_ref, wup_ref, cw_ref, cb_ref, wdn_ref, gpost_ref, o_ref, he_ref, acc_ref,
                *, d_ff, cols, sub):
    t = pl.program_id(1)
    tm = h_ref.shape[0]
    halo = FFN_HALO
    he_ref[0:halo, :] = jnp.where(t > 0, hh_ref[...], jnp.zeros_like(hh_ref))
    he_ref[halo:, :] = h_ref[...]

    def up(item):
        r0, c0 = item
        he = he_ref[r0:r0 + halo + sub, :]
        return [jnp.dot(he, wup_ref[:, base:base + cols], preferred_element_type=F32)
                for base in (c0, d_ff + c0)]

    def gate(item, us):
        _, c0 = item
        branch = []
        for base, u in zip((c0, d_ff + c0), us):
            cs = slice(base, base + cols)
            w = cw_ref[:, cs]
            b = cb_ref[:, cs]
            parts = []
            for p0 in range(0, sub, PERM):
                blk = u[halo + p0:halo + p0 + PERM]
                parts.append(_perm_conv(blk, u[p0:p0 + halo], w, b))
            branch.append(jnp.concatenate(parts, axis=0))
        return (jax.nn.gelu(branch[0], approximate=True) * branch[1]).astype(BF16)

    items = [(r0, c0) for r0 in range(0, tm, sub) for c0 in range(0, d_ff, cols)]
    us = up(items[0])
    for i, item in enumerate(items):
        us_next = up(items[i + 1]) if i + 1 < len(items) else None
        r0, c0 = item
        rs = slice(r0, r0 + sub)
        down = jnp.dot(gate(item, us), wdn_ref[c0:c0 + cols, :], preferred_element_type=F32)
        if c0 == 0:
            acc_ref[rs, :] = down
        else:
            acc_ref[rs, :] += down
        if c0 + cols >= d_ff:
            o_ref[rs, :] = x_ref[rs, :] + _rms(acc_ref[rs, :], gpost_ref[...])
        us = us_next


def _ffn(h2, x1, wup, cw, cb, wdn, gpost, *, batch, seq):
    t, d = x1.shape
    d_ff = wdn.shape[0]
    tm = ROWS_FFN
    nt = seq // tm
    assert FFN_HALO == SUBLANES * (cw.shape[0] - 1)
    rows = lambda b, i: (b * nt + i, 0)
    halo = lambda b, i: (jnp.maximum((b * seq + i * tm) // FFN_HALO - 1, 0), 0)
    const = lambda b, i: (0, 0)
    return pl.pallas_call(
        functools.partial(_ffn_kernel, d_ff=d_ff, cols=FFN_COLS, sub=FFN_SUB),
        grid=(batch, nt),
        in_specs=[
            pl.BlockSpec((tm, d), rows),
            pl.BlockSpec((FFN_HALO, d), halo),
            pl.BlockSpec((tm, d), rows),
            pl.BlockSpec(wup.shape, const),
            pl.BlockSpec(cw.shape, const),
            pl.BlockSpec(cb.shape, const),
            pl.BlockSpec(wdn.shape, const),
            pl.BlockSpec((1, d), const),
        ],
        out_specs=pl.BlockSpec((tm, d), rows),
        out_shape=jax.ShapeDtypeStruct((t, d), F32),
        scratch_shapes=[pltpu.VMEM((FFN_HALO + tm, d), BF16), pltpu.VMEM((tm, d), F32)],
        compiler_params=pltpu.CompilerParams(
            dimension_semantics=("arbitrary", "arbitrary"), vmem_limit_bytes=VMEM_LIMIT),
        name="ffn",
    )(h2, h2, x1, wup, cw, cb, wdn, gpost)


def _cast_kernel(w_ref, o_ref):
    o_ref[...] = w_ref[...].astype(o_ref.dtype)


def _layer_bf16(w, layer):
    _, r, c = w.shape
    tr = ROWS_CAST
    return pl.pallas_call(
        _cast_kernel,
        grid=(r // tr,),
        in_specs=[pl.BlockSpec((None, tr, c), lambda i: (layer, i, 0))],
        out_specs=pl.BlockSpec((tr, c), lambda i: (i, 0)),
        out_shape=jax.ShapeDtypeStruct((r, c), BF16),
        compiler_params=pltpu.CompilerParams(
            dimension_semantics=("arbitrary",), vmem_limit_bytes=VMEM_LIMIT),
        name="cast",
    )(w)


def _regroup_kernel(w_ref, o_ref, *, groups):
    c = 0
    for a, b in groups:
        o_ref[:, c:c + b - a] = w_ref[:, a:b].astype(o_ref.dtype)
        c += b - a
    o_ref[:, c:] = jnp.zeros((o_ref.shape[0], o_ref.shape[1] - c), o_ref.dtype)


def _regrouped_bf16(w, layer, groups, n_out):
    _, r, c = w.shape
    tr = ROWS_CAST
    return pl.pallas_call(
        functools.partial(_regroup_kernel, groups=groups),
        grid=(r // tr,),
        in_specs=[pl.BlockSpec((None, tr, c), lambda i: (layer, i, 0))],
        out_specs=pl.BlockSpec((tr, n_out), lambda i: (i, 0)),
        out_shape=jax.ShapeDtypeStruct((r, n_out), BF16),
        compiler_params=pltpu.CompilerParams(
            dimension_semantics=("arbitrary",), vmem_limit_bytes=VMEM_LIMIT),
        name="regroup",
    )(w)


def _block_diag(w, per_tile):
    nb, e, _ = w.shape
    w = w.reshape(nb // per_tile, per_tile, e, e)
    eye = jnp.eye(per_tile, dtype=w.dtype)
    out = jnp.einsum("tpij,pq->tpiqj", w, eye)
    return out.reshape(nb // per_tile, per_tile * e, per_tile * e)


def _to_segment_order(x):
    b, s, d = x.shape
    x = x.reshape(b, s // PERM, SUBLANES, PERM // SUBLANES, d)
    return jnp.swapaxes(x, 2, 3).reshape(b * s, d)


def _from_segment_order(x2, b, s):
    d = x2.shape[1]
    x = x2.reshape(b, s // PERM, PERM // SUBLANES, SUBLANES, d)
    return jnp.swapaxes(x, 2, 3).reshape(b, s, d)


def kernel(x, norm_mix_pre, norm_mix_post, norm_ffn_pre, norm_ffn_post, w_in, conv_m_w, conv_m_b, w_q_m, w_k_m, w_v_m, b_i_m, b_f_m, norm_m, conv_r_w, conv_r_b, w_a_r, b_a_r, w_x_r, b_x_r, lam_r, conv_s_w, conv_s_b, dt_bias_s, a_log_s, d_skip_s, norm_s, w_out, w_up, conv_f_w, conv_f_b, w_down):
    batch, seq, d = x.shape
    depth = w_in.shape[0]
    dm = conv_m_w.shape[2]
    dr = conv_r_w.shape[2]
    ds = norm_s.shape[1]
    dconv = conv_s_w.shape[2]
    hm = b_i_m.shape[1]
    hs = dt_bias_s.shape[1]
    sizes = (dm, dm, hm, hm, dr, dr, ds, dconv, hs)
    offs = [0]
    for s in sizes:
        offs.append(offs[-1] + s)
    col = lambda j: (offs[j], offs[j + 1])
    main_cols = [col(0), col(4), col(7), col(1), col(5), col(6)]
    n_main = sum(b - a for a, b in main_cols)

    row = lambda v: v.reshape(1, -1).astype(F32)
    x2 = _to_segment_order(x)
    for l in range(depth):
        w_all = _regrouped_bf16(w_in, l, tuple(main_cols) + ((offs[2], offs[4]), (offs[8], offs[9])),
                                n_main + LANES)
        gate_b = jnp.concatenate([b_i_m[l], b_f_m[l], dt_bias_s[l],
                                  jnp.zeros((LANES - 2 * hm - hs,), F32)]).reshape(1, LANES)
        alog = jnp.concatenate([jnp.zeros((GATE_DT,), F32), a_log_s[l],
                                jnp.zeros((LANES - GATE_DT - hs,), F32)]).reshape(1, LANES)
        cw = jnp.concatenate([conv_m_w[l], conv_r_w[l], conv_s_w[l]], axis=1)
        cb = jnp.concatenate([conv_m_b[l], conv_r_b[l], conv_s_b[l]]).reshape(1, -1)

        x1, h2 = _front(x2, row(norm_mix_pre[l]), w_all, cw, cb, jnp.concatenate([w_q_m[l], w_k_m[l]], axis=-1).astype(BF16),
                        w_v_m[l].astype(BF16), gate_b, row(norm_m[l]), _block_diag(w_a_r[l], 4).astype(BF16),
                        _block_diag(w_x_r[l], 4).astype(BF16), row(b_a_r[l]), row(b_x_r[l]), row(lam_r[l]),
                        alog, row(jnp.repeat(d_skip_s[l], SSD_HEAD_DIM)), row(norm_s[l]),
                        _layer_bf16(w_out, l), row(norm_mix_post[l]), row(norm_ffn_pre[l]),
                        batch=batch, seq=seq, n_main=n_main)
        x2 = _ffn(h2, x1, _layer_bf16(w_up, l), conv_f_w[l], row(conv_f_b[l]),
                  _layer_bf16(w_down, l), row(norm_ffn_post[l]), batch=batch, seq=seq)
    return _from_segment_order(x2, batch, seq)
```

```python
import functools

import jax
import jax.numpy as jnp
from jax import lax
from jax.experimental import pallas as pl
from jax.experimental.pallas import tpu as pltpu

F32 = jnp.float32
BF16 = jnp.bfloat16
EPS = 1e-6

LANES = 128
SUBLANES = 8
VMEM_LIMIT = 56 * 1024 * 1024

MLSTM_HEADS = 4
RGLRU_C = 8.0
SSD_HEAD_DIM = 64
SSD_GROUPS = 2
SSD_STATE = 128
GATE_I, GATE_F, GATE_DT = 0, 4, 8

PERM = 128
ROWS_FRONT = 256
FRONT_COLS = 256
ROWS_FFN = 1024
FFN_SUB = 256
ROWS_CAST = 256
FFN_COLS = 256
FFN_HALO = 16


def _rms(x, g):
    return x * lax.rsqrt(jnp.mean(x * x, axis=-1, keepdims=True) + EPS) * g


def _softplus(x):
    return jnp.maximum(x, 0.0) + jnp.log(1.0 + jnp.exp(-jnp.abs(x)))


def _sigmoid(x):
    return 0.5 * jnp.tanh(0.5 * x) + 0.5


def _silu(x):
    hx = 0.5 * x
    return hx * jnp.tanh(hx) + hx


def _wrap_tail(prev_tail, tail):
    out = []
    for j in range(tail.shape[0] // SUBLANES):
        rs = slice(j * SUBLANES, (j + 1) * SUBLANES)
        sub = lax.broadcasted_iota(jnp.int32, (SUBLANES, tail.shape[1]), 0)
        mixed = jnp.where(sub == SUBLANES - 1, prev_tail[rs], tail[rs])
        out.append(pltpu.roll(mixed, 1, 0))
    return jnp.concatenate(out, axis=0)


def _perm_conv(x, prev_tail, w, b):
    n = x.shape[0]
    k = w.shape[0]
    nt = SUBLANES * (k - 1)
    ext = jnp.concatenate([_wrap_tail(prev_tail, x[n - nt:]), x], axis=0)
    out = b + w[k - 1:k] * x
    for j in range(1, k):
        start = nt - SUBLANES * j
        out = out + w[k - 1 - j:k - j] * ext[start:start + n]
    return out


def _causal_mask(n):
    seg = n // SUBLANES
    row = lax.broadcasted_iota(jnp.int32, (n, n), 0)
    col = lax.broadcasted_iota(jnp.int32, (n, n), 1)
    t_row = (row & (SUBLANES - 1)) * seg + (row >> 3)
    t_col = (col & (SUBLANES - 1)) * seg + (col >> 3)
    return t_row >= t_col


def _time_cumsum(tri_bf, steps):
    c = steps.shape[1]
    hi = steps.astype(BF16)
    rest = steps - hi.astype(F32)
    mid = rest.astype(BF16)
    lo = (rest - mid.astype(F32)).astype(BF16)
    top = jnp.dot(tri_bf, jnp.concatenate([hi, mid], axis=1), preferred_element_type=F32)
    return top[:, :c] + (top[:, c:] + jnp.dot(tri_bf, lo, preferred_element_type=F32))


def _lane_pair(col_lo, col_hi, n, width):
    lane = lax.broadcasted_iota(jnp.int32, (n, width), 1)
    return jnp.where(lane < width // 2, col_lo, col_hi)


def _mlstm_project(xc, xm_bf, wqk_ref, wv_ref, h, *, n, dh):
    sl = slice(h * dh, (h + 1) * dh)
    qk = jnp.dot(xc[:, sl].astype(BF16), wqk_ref[h], preferred_element_type=F32)
    v = jnp.dot(xm_bf[:, sl], wv_ref[h], preferred_element_type=F32).astype(BF16)
    return qk[:, :dh].astype(BF16), qk[:, dh:] * (dh ** -0.5), jnp.concatenate([v, jnp.ones((n, dh), BF16)], axis=1)


def _mlstm_decay(g, cs, cs_t, g_t, causal, m_prev, h, *, n):
    bcol = cs[:, GATE_F + h:GATE_F + h + 1]
    brow = cs_t[GATE_F + h:GATE_F + h + 1, :]
    icol = g[:, GATE_I + h:GATE_I + h + 1]
    irow = g_t[GATE_I + h:GATE_I + h + 1, :]
    btot = cs[n - 1:n, GATE_F + h:GATE_F + h + 1]
    d = jnp.where(causal, bcol - brow + irow, -jnp.inf)
    inter = bcol + m_prev
    m_t = jnp.maximum(inter, jnp.max(d, axis=1, keepdims=True))
    wst = btot - bcol + icol
    m_loc = jnp.max(wst, axis=0, keepdims=True)
    m_new = jnp.maximum(btot + m_prev, m_loc)
    return dict(pmat=jnp.exp(d - m_t), e_inter=jnp.exp(inter - m_t), floor=jnp.exp(-m_t),
                ew=jnp.exp(wst - m_loc), s_prev=jnp.exp(btot + m_prev - m_new),
                s_loc=jnp.exp(m_loc - m_new), m_new=m_new)


def _mlstm_output(q, k, v_aug, dec, c_prev, *, dh):
    scores = lax.dot_general(q, k.astype(BF16), (((1,), (1,)), ((), ())),
                             preferred_element_type=F32) * dec["pmat"]
    comb = (jnp.dot(scores.astype(BF16), v_aug, preferred_element_type=F32)
            + dec["e_inter"] * jnp.dot(q, c_prev.astype(BF16), preferred_element_type=F32))
    hh = comb[:, :dh] / jnp.maximum(jnp.abs(comb[:, dh:]), dec["floor"])
    ek = (dec["ew"] * k).astype(BF16)
    c_loc = lax.dot_general(ek, v_aug, (((0,), (0,)), ((), ())), preferred_element_type=F32)
    return hh, dec["s_prev"] * c_prev + dec["s_loc"] * c_loc


def _rglru_scan(a, u, carry):
    n, d = a.shape
    tiles = n // SUBLANES
    hs, ps = [u[0:SUBLANES]], [a[0:SUBLANES]]
    for i in range(1, tiles):
        rs = slice(i * SUBLANES, (i + 1) * SUBLANES)
        hs.append(a[rs] * hs[-1] + u[rs])
        ps.append(a[rs] * ps[-1])
    gacc, pacc = hs[-1], ps[-1]
    sub = lax.broadcasted_iota(jnp.int32, (SUBLANES, d), 0)
    for s in (1, 2, 4):
        keep = sub >= s
        gacc = jnp.where(keep, pacc * pltpu.roll(gacc, s, 0) + gacc, gacc)
        pacc = jnp.where(keep, pacc * pltpu.roll(pacc, s, 0), pacc)
    seg_end = gacc + pacc * carry
    seg_in = jnp.where(sub == 0, carry, pltpu.roll(seg_end, 1, 0))
    out = [hs[i] + ps[i] * seg_in for i in range(tiles)]
    return jnp.concatenate(out, axis=0), seg_end[SUBLANES - 1:SUBLANES]


def _ssd_decay(dt, acs, acs_t, causal, h0, *, n, hd):
    pair = 2 * hd
    cols = [GATE_DT + h0, GATE_DT + h0 + 1]
    acol = [acs[:, c:c + 1] for c in cols]
    arow = [acs_t[c:c + 1, :] for c in cols]
    atot = [acs[n - 1:n, c:c + 1] for c in cols]
    return dict(
        dec=[jnp.exp(jnp.where(causal, acol[e] - arow[e], -jnp.inf)) for e in range(2)],
        dt=_lane_pair(dt[:, cols[0]:cols[0] + 1], dt[:, cols[1]:cols[1] + 1], n, pair),
        e_in=jnp.exp(_lane_pair(acol[0], acol[1], n, pair)),
        e_out=jnp.exp(_lane_pair(atot[0] - acol[0], atot[1] - acol[1], n, pair)),
        e_tot=jnp.exp(_lane_pair(atot[0], atot[1], 1, pair)))


def _ssd_output(xs_p, cbm, dec, yoff_p, dskip_p, *, n, hd):
    pair = 2 * hd
    lo = lax.broadcasted_iota(jnp.int32, (n, pair), 1) < hd
    xdt = xs_p * dec["dt"]
    ydiag = jnp.zeros((n, pair), F32)
    for e in range(2):
        half = jnp.where(lo if e == 0 else jnp.logical_not(lo), xdt, 0.0).astype(BF16)
        ydiag = ydiag + jnp.dot((cbm * dec["dec"][e]).astype(BF16), half, preferred_element_type=F32)
    return ydiag + yoff_p * dec["e_in"] + xs_p * dskip_p, (xdt * dec["e_out"]).astype(BF16)


def _front_kernel(xn_ref, xres_ref, gin_ref, win_ref, cw_ref, cb_ref, wqk_ref, wv_ref, gb_ref, gnm_ref,
                  wa_ref, wx_ref, ba_ref, bx_ref, lam_ref, alog_ref, dskip_ref, gns_ref,
                  wo_ref, gpost_ref, gffn_ref, x1_ref, h2_ref,
                  pbuf, ybuf, tail_ref, c_ref, m_ref, h_ref, s_ref, *, n, steps_per_seq, dm, dr, ds, n_main):
    step = pl.program_id(0)

    def reset_state():
        tail_ref[...] = jnp.zeros_like(tail_ref)
        c_ref[...] = jnp.zeros_like(c_ref)
        m_ref[...] = jnp.zeros_like(m_ref)
        h_ref[...] = jnp.zeros_like(h_ref)
        s_ref[...] = jnp.zeros_like(s_ref)

    @pl.when(step == 0)
    def _first():
        pbuf[1] = jnp.zeros(pbuf.shape[1:], pbuf.dtype)
        ybuf[...] = jnp.zeros_like(ybuf)
        reset_state()

    for half in range(2):
        if half == 1:
            pl.when(lax.rem(2 * step, steps_per_seq) == 0)(reset_state)
        blk = slice(half * n, (half + 1) * n)
        _front_block(xn_ref.at[blk], xres_ref, gin_ref, win_ref, cw_ref, cb_ref, wqk_ref, wv_ref, gb_ref,
                     gnm_ref, wa_ref, wx_ref, ba_ref, bx_ref, lam_ref, alog_ref, dskip_ref, gns_ref, wo_ref,
                     gpost_ref, gffn_ref, x1_ref, h2_ref, pbuf.at[half], pbuf.at[1 - half], ybuf,
                     ybuf.at[slice((1 - half) * n, (2 - half) * n)],
                     tail_ref, c_ref, m_ref, h_ref, s_ref, n=n, dm=dm, dr=dr, ds=ds, n_main=n_main,
                     out_project=(half == 1))


def _front_block(xn_ref, xres_ref, gin_ref, win_ref, cw_ref, cb_ref, wqk_ref, wv_ref, gb_ref, gnm_ref,
                 wa_ref, wx_ref, ba_ref, bx_ref, lam_ref, alog_ref, dskip_ref, gns_ref,
                 wo_ref, gpost_ref, gffn_ref, x1_ref, h2_ref, p_out, p_ref,
                 y_all, ybuf, tail_ref, c_ref, m_ref, h_ref, s_ref, *, n, dm, dr, ds, n_main, out_project):
    heads_m = MLSTM_HEADS
    dh = dm // heads_m
    hd = SSD_HEAD_DIM
    pair = 2 * hd
    n_pairs = ds // pair
    pairs_per_group = n_pairs // SSD_GROUPS
    n_all = win_ref.shape[1]
    n_conv = dm + dr + 2 * ds
    o0 = n_conv
    ns = SSD_STATE

    if out_project:
        mix = jnp.dot(y_all[...], wo_ref[...], preferred_element_type=F32)

    h_next = _rms(xn_ref[...], gin_ref[...]).astype(BF16)
    starts = list(range(0, n_all, FRONT_COLS))
    chunks = n // PERM
    per_call = -(-len(starts) // (8 * chunks))

    def project(k=per_call):
        for _ in range(k):
            if starts:
                c0 = starts.pop(0)
                c1 = min(c0 + FRONT_COLS, n_all)
                p_out[:, c0:c1] = jnp.dot(h_next, win_ref[:, c0:c1], preferred_element_type=F32)

    causal = _causal_mask(PERM)
    tri = jnp.where(causal, 1.0, 0.0).astype(BF16)
    lane = lax.broadcasted_iota(jnp.int32, (1, LANES), 1)

    for ch in range(chunks):
        rows = slice(ch * PERM, (ch + 1) * PERM)

        g = p_ref[rows, n_main:] + gb_ref[...]
        sp = _softplus(jnp.where(lane < GATE_DT, -g, g))
        dt = sp
        steps = jnp.where(lane < GATE_DT, -sp, sp * (-jnp.exp(alog_ref[...])))
        cs = _time_cumsum(tri, steps)
        project()

        nt_rows = tail_ref.shape[0]
        x_in = p_ref[rows, :n_conv]
        conv = _perm_conv(x_in, tail_ref[...], cw_ref[...], cb_ref[...])
        tail_ref[...] = x_in[PERM - nt_rows:]
        project()
        xc_m = _silu(conv[:, :dm])
        xc_r = conv[:, dm:dm + dr]
        xs_bc = _silu(conv[:, dm + dr:])
        cs_t, g_t = cs.T, g.T

        if ch == 0 and out_project:
            x1 = xres_ref[...] + _rms(mix, gpost_ref[...])
            x1_ref[...] = x1
            h2_ref[...] = _rms(x1, gffn_ref[...]).astype(h2_ref.dtype)
        project()

        xm_bf = x_in[:, :dm].astype(BF16)
        qkv = [_mlstm_project(xc_m, xm_bf, wqk_ref, wv_ref, h, n=PERM, dh=dh) for h in range(heads_m)]
        xr_bf = xc_r.astype(BF16)
        tile = wa_ref.shape[1]
        ra = [jnp.dot(xr_bf[:, j * tile:(j + 1) * tile], wa_ref[j], preferred_element_type=F32)
              for j in range(dr // tile)]
        ri = [jnp.dot(xr_bf[:, j * tile:(j + 1) * tile], wx_ref[j], preferred_element_type=F32)
              for j in range(dr // tile)]
        b_g = [xs_bc[:, ds + grp * ns:ds + (grp + 1) * ns].astype(BF16) for grp in range(SSD_GROUPS)]
        c_g = [xs_bc[:, ds + (SSD_GROUPS + grp) * ns:ds + (SSD_GROUPS + grp + 1) * ns].astype(BF16)
               for grp in range(SSD_GROUPS)]
        cbm = [lax.dot_general(c_g[grp], b_g[grp], (((1,), (1,)), ((), ())), preferred_element_type=F32)
               for grp in range(SSD_GROUPS)]

        r = _sigmoid(jnp.concatenate(ra, axis=1) + ba_ref[...])
        i = _sigmoid(jnp.concatenate(ri, axis=1) + bx_ref[...])
        a = jnp.exp((-RGLRU_C) * r * _softplus(-lam_ref[...]))
        w = 1.0 - a * a
        u = jnp.where(w > 0.0, w * lax.rsqrt(w), 0.0) * (i * xc_r)
        project()
        hseq, h_ref[...] = _rglru_scan(a, u, h_ref[...])
        yr = p_ref[rows, o0 + dm:o0 + dm + dr]
        ybuf[rows, dm:dm + dr] = (hseq * jax.nn.gelu(yr, approximate=True)).astype(ybuf.dtype)
        project()

        dec_m = [_mlstm_decay(g, cs, cs_t, g_t, causal, m_ref[h:h + 1, 0:1], h, n=PERM) for h in range(heads_m)]
        dec_s = [_ssd_decay(dt, cs, cs_t, causal, 2 * p, n=PERM, hd=hd) for p in range(n_pairs)]
        gw = pairs_per_group * pair
        yoff = [jnp.dot(c_g[grp], s_ref[:, grp * gw:(grp + 1) * gw].astype(BF16), preferred_element_type=F32)
                for grp in range(SSD_GROUPS)]
        ys, xds = [], []
        for j in range(max(heads_m, n_pairs)):
            if j < heads_m:
                sl = slice(j * dh, (j + 1) * dh)
                hh, c_ref[j] = _mlstm_output(*qkv[j], dec_m[j], c_ref[j], dh=dh)
                m_ref[j:j + 1, :] = jnp.broadcast_to(dec_m[j]["m_new"], (1, LANES))
                o = _sigmoid(p_ref[rows, o0 + j * dh:o0 + (j + 1) * dh]) * hh
                ybuf[rows, sl] = _rms(o, gnm_ref[:, sl]).astype(ybuf.dtype)
            if j % 2 == 1:
                project()
            if j < n_pairs:
                ps = slice(j * pair, (j + 1) * pair)
                grp, jg = divmod(j, pairs_per_group)
                y_p, xd_p = _ssd_output(xs_bc[:, ps], cbm[grp], dec_s[j], yoff[grp][:, jg * pair:(jg + 1) * pair],
                                        dskip_ref[:, ps], n=PERM, hd=hd)
                ys.append(y_p)
                xds.append(xd_p)
                if jg == pairs_per_group - 1:
                    gs = slice(grp * gw, (grp + 1) * gw)
                    s_loc = lax.dot_general(b_g[grp], jnp.concatenate(xds[-pairs_per_group:], axis=1),
                                            (((0,), (0,)), ((), ())), preferred_element_type=F32)
                    e_tot = jnp.concatenate([dec_s[p]["e_tot"] for p in range(j + 1 - pairs_per_group, j + 1)],
                                            axis=1)
                    s_ref[:, gs] = e_tot * s_ref[:, gs] + s_loc
        project()
        z = p_ref[rows, o0 + dm + dr:o0 + dm + dr + ds]
        ybuf[rows, dm + dr:] = _rms(jnp.concatenate(ys, axis=1) * _silu(z), gns_ref[...]).astype(ybuf.dtype)
    project(len(starts))


def _front(x2, gin, win, cw, cb, wqk, wv, gb, gnm, wa, wx, ba, bx, lam, alog, dskip, gns, wo, gpost, gffn,
           *, batch, seq, n_main):
    n = ROWS_FRONT
    t, d = x2.shape
    dm, dr, ds = gnm.shape[1], lam.shape[1], gns.shape[1]
    nt = seq // n
    nblk = batch * nt
    k = cw.shape[0]
    consts = [gin, win, cw, cb, wqk, wv, gb, gnm, wa, wx, ba, bx, lam, alog, dskip, gns, wo, gpost, gffn]
    const_spec = lambda a: pl.BlockSpec(a.shape, lambda s, _nd=a.ndim: (0,) * _nd)
    nstep = nblk // 2
    nxt = lambda s: (jnp.minimum(s, nstep - 1), 0)
    done = lambda s: (jnp.clip(s - 1, 0, nstep - 1), 0)
    return pl.pallas_call(
        functools.partial(_front_kernel, n=n, steps_per_seq=nt, dm=dm, dr=dr, ds=ds, n_main=n_main),
        grid=(nstep + 1,),
        in_specs=[pl.BlockSpec((2 * n, d), nxt), pl.BlockSpec((2 * n, d), done)] + [const_spec(a) for a in consts],
        out_specs=[pl.BlockSpec((2 * n, d), done), pl.BlockSpec((2 * n, d), done)],
        out_shape=[jax.ShapeDtypeStruct((t, d), F32), jax.ShapeDtypeStruct((t, d), BF16)],
        scratch_shapes=[
            pltpu.VMEM((2, n, win.shape[1]), F32),
            pltpu.VMEM((2 * n, dm + dr + ds), BF16),
            pltpu.VMEM((SUBLANES * (k - 1), cw.shape[1]), F32),
            pltpu.VMEM((MLSTM_HEADS, dm // MLSTM_HEADS, 2 * dm // MLSTM_HEADS), F32),
            pltpu.VMEM((SUBLANES, LANES), F32),
            pltpu.VMEM((1, dr), F32),
            pltpu.VMEM((SSD_STATE, ds), F32),
        ],
        compiler_params=pltpu.CompilerParams(
            dimension_semantics=("arbitrary",), vmem_limit_bytes=VMEM_LIMIT),
        name="front",
    )(x2, x2, *consts)


def _ffn_kernel(h_ref, hh_ref, x_ref, wup_ref, cw_ref, cb_ref, wdn_ref, gpost_ref, o_ref, he_ref, acc_ref,
                *, d_ff, cols, sub):
    t = pl.program_id(1)
    tm = h_ref.shape[0]
    halo = FFN_HALO
    he_ref[0:halo, :] = jnp.where(t > 0, hh_ref[...], jnp.zeros_like(hh_ref))
    he_ref[halo:, :] = h_ref[...]

    def up(item):
        r0, c0 = item
        he = he_ref[r0:r0 + halo + sub, :]
        return [jnp.dot(he, wup_ref[:, base:base + cols], preferred_element_type=F32)
                for base in (c0, d_ff + c0)]

    def gate(item, us):
        _, c0 = item
        branch = []
        for base, u in zip((c0, d_ff + c0), us):
            cs = slice(base, base + cols)
            w = cw_ref[:, cs]
            b = cb_ref[:, cs]
            parts = []
            for p0 in range(0, sub, PERM):
                blk = u[halo + p0:halo + p0 + PERM]
                parts.append(_perm_conv(blk, u[p0:p0 + halo], w, b))
            branch.append(jnp.concatenate(parts, axis=0))
        return (jax.nn.gelu(branch[0], approximate=True) * branch[1]).astype(BF16)

    items = [(r0, c0) for r0 in range(0, tm, sub) for c0 in range(0, d_ff, cols)]
    us = up(items[0])
    for i, item in enumerate(items):
        us_next = up(items[i + 1]) if i + 1 < len(items) else None
        r0, c0 = item
        rs = slice(r0, r0 + sub)
        down = jnp.dot(gate(item, us), wdn_ref[c0:c0 + cols, :], preferred_element_type=F32)
        if c0 == 0:
            acc_ref[rs, :] = down
        else:
            acc_ref[rs, :] += down
        if c0 + cols >= d_ff:
            o_ref[rs, :] = x_ref[rs, :] + _rms(acc_ref[rs, :], gpost_ref[...])
        us = us_next


def _ffn(h2, x1, wup, cw, cb, wdn, gpost, *, batch, seq):
    t, d = x1.shape
    d_ff = wdn.shape[0]
    tm = ROWS_FFN
    nt = seq // tm
    assert FFN_HALO == SUBLANES * (cw.shape[0] - 1)
    rows = lambda b, i: (b * nt + i, 0)
    halo = lambda b, i: (jnp.maximum((b * seq + i * tm) // FFN_HALO - 1, 0), 0)
    const = lambda b, i: (0, 0)
    return pl.pallas_call(
        functools.partial(_ffn_kernel, d_ff=d_ff, cols=FFN_COLS, sub=FFN_SUB),
        grid=(batch, nt),
        in_specs=[
            pl.BlockSpec((tm, d), rows),
            pl.BlockSpec((FFN_HALO, d), halo),
            pl.BlockSpec((tm, d), rows),
            pl.BlockSpec(wup.shape, const),
            pl.BlockSpec(cw.shape, const),
            pl.BlockSpec(cb.shape, const),
            pl.BlockSpec(wdn.shape, const),
            pl.BlockSpec((1, d), const),
        ],
        out_specs=pl.BlockSpec((tm, d), rows),
        out_shape=jax.ShapeDtypeStruct((t, d), F32),
        scratch_shapes=[pltpu.VMEM((FFN_HALO + tm, d), BF16), pltpu.VMEM((tm, d), F32)],
        compiler_params=pltpu.CompilerParams(
            dimension_semantics=("arbitrary", "arbitrary"), vmem_limit_bytes=VMEM_LIMIT),
        name="ffn",
    )(h2, h2, x1, wup, cw, cb, wdn, gpost)


def _cast_kernel(w_ref, o_ref):
    o_ref[...] = w_ref[...].astype(o_ref.dtype)


def _layer_bf16(w, layer):
    _, r, c = w.shape
    tr = next(rows for rows in (2 * ROWS_CAST, ROWS_CAST) if r % rows == 0)
    return pl.pallas_call(
        _cast_kernel,
        grid=(r // tr,),
        in_specs=[pl.BlockSpec((None, tr, c), lambda i: (layer, i, 0))],
        out_specs=pl.BlockSpec((tr, c), lambda i: (i, 0)),
        out_shape=jax.ShapeDtypeStruct((r, c), BF16),
        compiler_params=pltpu.CompilerParams(
            dimension_semantics=("arbitrary",), vmem_limit_bytes=VMEM_LIMIT),
        name="cast",
    )(w)


def _block_diag(w, per_tile):
    nb, e, _ = w.shape
    w = w.reshape(nb // per_tile, per_tile, e, e)
    eye = jnp.eye(per_tile, dtype=w.dtype)
    out = jnp.einsum("tpij,pq->tpiqj", w, eye)
    return out.reshape(nb // per_tile, per_tile * e, per_tile * e)


def _to_segment_order(x):
    b, s, d = x.shape
    x = x.reshape(b, s // PERM, SUBLANES, PERM // SUBLANES, d)
    return jnp.swapaxes(x, 2, 3).reshape(b * s, d)


def _from_segment_order(x2, b, s):
    d = x2.shape[1]
    x = x2.reshape(b, s // PERM, PERM // SUBLANES, SUBLANES, d)
    return jnp.swapaxes(x, 2, 3).reshape(b, s, d)


def kernel(x, norm_mix_pre, norm_mix_post, norm_ffn_pre, norm_ffn_post, w_in, conv_m_w, conv_m_b, w_q_m, w_k_m, w_v_m, b_i_m, b_f_m, norm_m, conv_r_w, conv_r_b, w_a_r, b_a_r, w_x_r, b_x_r, lam_r, conv_s_w, conv_s_b, dt_bias_s, a_log_s, d_skip_s, norm_s, w_out, w_up, conv_f_w, conv_f_b, w_down):
    batch, seq, d = x.shape
    depth = w_in.shape[0]
    dm = conv_m_w.shape[2]
    dr = conv_r_w.shape[2]
    ds = norm_s.shape[1]
    dconv = conv_s_w.shape[2]
    hm = b_i_m.shape[1]
    hs = dt_bias_s.shape[1]
    sizes = (dm, dm, hm, hm, dr, dr, ds, dconv, hs)
    offs = [0]
    for s in sizes:
        offs.append(offs[-1] + s)
    col = lambda j: (offs[j], offs[j + 1])
    main_cols = [col(0), col(4), col(7), col(1), col(5), col(6)]
    n_main = sum(b - a for a, b in main_cols)

    row = lambda v: v.reshape(1, -1).astype(F32)
    x2 = _to_segment_order(x)
    w_in_bf = w_in.astype(BF16)
    for l in range(depth):
        wl = w_in_bf[l]
        gate_w = jnp.concatenate(
            [wl[:, offs[2]:offs[4]], wl[:, offs[8]:offs[9]],
             jnp.zeros((d, LANES - 2 * hm - hs), BF16)], axis=1)
        w_all = jnp.concatenate([wl[:, a:b] for a, b in main_cols] + [gate_w], axis=1)
        gate_b = jnp.concatenate([b_i_m[l], b_f_m[l], dt_bias_s[l],
                                  jnp.zeros((LANES - 2 * hm - hs,), F32)]).reshape(1, LANES)
        alog = jnp.concatenate([jnp.zeros((GATE_DT,), F32), a_log_s[l],
                                jnp.zeros((LANES - GATE_DT - hs,), F32)]).reshape(1, LANES)
        cw = jnp.concatenate([conv_m_w[l], conv_r_w[l], conv_s_w[l]], axis=1)
        cb = jnp.concatenate([conv_m_b[l], conv_r_b[l], conv_s_b[l]]).reshape(1, -1)

        x1, h2 = _front(x2, row(norm_mix_pre[l]), w_all, cw, cb, jnp.concatenate([w_q_m[l], w_k_m[l]], axis=-1).astype(BF16),
                        w_v_m[l].astype(BF16), gate_b, row(norm_m[l]), _block_diag(w_a_r[l], 4).astype(BF16),
                        _block_diag(w_x_r[l], 4).astype(BF16), row(b_a_r[l]), row(b_x_r[l]), row(lam_r[l]),
                        alog, row(jnp.repeat(d_skip_s[l], SSD_HEAD_DIM)), row(norm_s[l]),
                        _layer_bf16(w_out, l), row(norm_mix_post[l]), row(norm_ffn_pre[l]),
                        batch=batch, seq=seq, n_main=n_main)
        x2 = _ffn(h2, x1, _layer_bf16(w_up, l), conv_f_w[l], row(conv_f_b[l]),
                  _layer_bf16(w_down, l), row(norm_ffn_post[l]), batch=batch, seq=seq)
    return _from_segment_order(x2, batch, seq)
```

```python
import functools

import jax
import jax.numpy as jnp
from jax import lax
from jax.experimental import pallas as pl
from jax.experimental.pallas import tpu as pltpu

F32 = jnp.float32
BF16 = jnp.bfloat16
EPS = 1e-6

LANES = 128
SUBLANES = 8
VMEM_LIMIT = 56 * 1024 * 1024

MLSTM_HEADS = 4
RGLRU_C = 8.0
SSD_HEAD_DIM = 64
SSD_GROUPS = 2
SSD_STATE = 128
GATE_I, GATE_F, GATE_DT = 0, 4, 8

PERM = 128
ROWS_FRONT = 256
FRONT_COLS = 256
ROWS_FFN = 1024
FFN_SUB = 256
ROWS_CAST = 256
FFN_COLS = 256
FFN_HALO = 16


def _rms(x, g):
    return x * lax.rsqrt(jnp.mean(x * x, axis=-1, keepdims=True) + EPS) * g


def _softplus(x):
    return jnp.maximum(x, 0.0) + jnp.log(1.0 + jnp.exp(-jnp.abs(x)))


def _sigmoid(x):
    return 0.5 * jnp.tanh(0.5 * x) + 0.5


def _silu(x):
    hx = 0.5 * x
    return hx * jnp.tanh(hx) + hx


def _wrap_tail(prev_tail, tail):
    out = []
    for j in range(tail.shape[0] // SUBLANES):
        rs = slice(j * SUBLANES, (j + 1) * SUBLANES)
        sub = lax.broadcasted_iota(jnp.int32, (SUBLANES, tail.shape[1]), 0)
        mixed = jnp.where(sub == SUBLANES - 1, prev_tail[rs], tail[rs])
        out.append(pltpu.roll(mixed, 1, 0))
    return jnp.concatenate(out, axis=0)


def _perm_conv(x, prev_tail, w, b):
    n = x.shape[0]
    k = w.shape[0]
    nt = SUBLANES * (k - 1)
    ext = jnp.concatenate([_wrap_tail(prev_tail, x[n - nt:]), x], axis=0)
    out = b + w[k - 1:k] * x
    for j in range(1, k):
        start = nt - SUBLANES * j
        out = out + w[k - 1 - j:k - j] * ext[start:start + n]
    return out


def _causal_mask(n):
    seg = n // SUBLANES
    row = lax.broadcasted_iota(jnp.int32, (n, n), 0)
    col = lax.broadcasted_iota(jnp.int32, (n, n), 1)
    t_row = (row & (SUBLANES - 1)) * seg + (row >> 3)
    t_col = (col & (SUBLANES - 1)) * seg + (col >> 3)
    return t_row >= t_col


def _time_cumsum(tri_bf, steps):
    c = steps.shape[1]
    hi = steps.astype(BF16)
    rest = steps - hi.astype(F32)
    mid = rest.astype(BF16)
    lo = (rest - mid.astype(F32)).astype(BF16)
    top = jnp.dot(tri_bf, jnp.concatenate([hi, mid], axis=1), preferred_element_type=F32)
    return top[:, :c] + (top[:, c:] + jnp.dot(tri_bf, lo, preferred_element_type=F32))


def _lane_pair(col_lo, col_hi, n, width):
    lane = lax.broadcasted_iota(jnp.int32, (n, width), 1)
    return jnp.where(lane < width // 2, col_lo, col_hi)


def _mlstm_project(xc, xm_bf, wqk_ref, wv_ref, h, *, n, dh):
    sl = slice(h * dh, (h + 1) * dh)
    qk = jnp.dot(xc[:, sl].astype(BF16), wqk_ref[h], preferred_element_type=F32)
    v = jnp.dot(xm_bf[:, sl], wv_ref[h], preferred_element_type=F32).astype(BF16)
    return qk[:, :dh].astype(BF16), qk[:, dh:] * (dh ** -0.5), jnp.concatenate([v, jnp.ones((n, dh), BF16)], axis=1)


def _mlstm_decay(g, cs, cs_t, g_t, causal, m_prev, h, *, n):
    bcol = cs[:, GATE_F + h:GATE_F + h + 1]
    brow = cs_t[GATE_F + h:GATE_F + h + 1, :]
    icol = g[:, GATE_I + h:GATE_I + h + 1]
    irow = g_t[GATE_I + h:GATE_I + h + 1, :]
    btot = cs[n - 1:n, GATE_F + h:GATE_F + h + 1]
    d = jnp.where(causal, bcol - brow + irow, -jnp.inf)
    inter = bcol + m_prev
    m_t = jnp.maximum(inter, jnp.max(d, axis=1, keepdims=True))
    wst = btot - bcol + icol
    m_loc = jnp.max(wst, axis=0, keepdims=True)
    m_new = jnp.maximum(btot + m_prev, m_loc)
    return dict(pmat=jnp.exp(d - m_t), e_inter=jnp.exp(inter - m_t), floor=jnp.exp(-m_t),
                ew=jnp.exp(wst - m_loc), s_prev=jnp.exp(btot + m_prev - m_new),
                s_loc=jnp.exp(m_loc - m_new), m_new=m_new)


def _mlstm_output(q, k, v_aug, dec, c_prev, *, dh):
    scores = lax.dot_general(q, k.astype(BF16), (((1,), (1,)), ((), ())),
                             preferred_element_type=F32) * dec["pmat"]
    comb = (jnp.dot(scores.astype(BF16), v_aug, preferred_element_type=F32)
            + dec["e_inter"] * jnp.dot(q, c_prev.astype(BF16), preferred_element_type=F32))
    hh = comb[:, :dh] / jnp.maximum(jnp.abs(comb[:, dh:]), dec["floor"])
    ek = (dec["ew"] * k).astype(BF16)
    c_loc = lax.dot_general(ek, v_aug, (((0,), (0,)), ((), ())), preferred_element_type=F32)
    return hh, dec["s_prev"] * c_prev + dec["s_loc"] * c_loc


def _rglru_scan(a, u, carry):
    n, d = a.shape
    tiles = n // SUBLANES
    hs, ps = [u[0:SUBLANES]], [a[0:SUBLANES]]
    for i in range(1, tiles):
        rs = slice(i * SUBLANES, (i + 1) * SUBLANES)
        hs.append(a[rs] * hs[-1] + u[rs])
        ps.append(a[rs] * ps[-1])
    gacc, pacc = hs[-1], ps[-1]
    sub = lax.broadcasted_iota(jnp.int32, (SUBLANES, d), 0)
    for s in (1, 2, 4):
        keep = sub >= s
        gacc = jnp.where(keep, pacc * pltpu.roll(gacc, s, 0) + gacc, gacc)
        pacc = jnp.where(keep, pacc * pltpu.roll(pacc, s, 0), pacc)
    seg_end = gacc + pacc * carry
    seg_in = jnp.where(sub == 0, carry, pltpu.roll(seg_end, 1, 0))
    out = [hs[i] + ps[i] * seg_in for i in range(tiles)]
    return jnp.concatenate(out, axis=0), seg_end[SUBLANES - 1:SUBLANES]


def _ssd_decay(dt, acs, acs_t, causal, h0, *, n, hd):
    pair = 2 * hd
    cols = [GATE_DT + h0, GATE_DT + h0 + 1]
    acol = [acs[:, c:c + 1] for c in cols]
    arow = [acs_t[c:c + 1, :] for c in cols]
    atot = [acs[n - 1:n, c:c + 1] for c in cols]
    return dict(
        dec=[jnp.exp(jnp.where(causal, acol[e] - arow[e], -jnp.inf)) for e in range(2)],
        dt=_lane_pair(dt[:, cols[0]:cols[0] + 1], dt[:, cols[1]:cols[1] + 1], n, pair),
        e_in=jnp.exp(_lane_pair(acol[0], acol[1], n, pair)),
        e_out=jnp.exp(_lane_pair(atot[0] - acol[0], atot[1] - acol[1], n, pair)),
        e_tot=jnp.exp(_lane_pair(atot[0], atot[1], 1, pair)))


def _ssd_output(xs_p, cbm, dec, yoff_p, dskip_p, *, n, hd):
    pair = 2 * hd
    lo = lax.broadcasted_iota(jnp.int32, (n, pair), 1) < hd
    xdt = xs_p * dec["dt"]
    ydiag = jnp.zeros((n, pair), F32)
    for e in range(2):
        half = jnp.where(lo if e == 0 else jnp.logical_not(lo), xdt, 0.0).astype(BF16)
        ydiag = ydiag + jnp.dot((cbm * dec["dec"][e]).astype(BF16), half, preferred_element_type=F32)
    return ydiag + yoff_p * dec["e_in"] + xs_p * dskip_p, (xdt * dec["e_out"]).astype(BF16)


def _front_kernel(xn_ref, xres_ref, gin_ref, win_ref, cw_ref, cb_ref, wqk_ref, wv_ref, gb_ref, gnm_ref,
                  wa_ref, wx_ref, ba_ref, bx_ref, lam_ref, alog_ref, dskip_ref, gns_ref,
                  wo_ref, gpost_ref, gffn_ref, x1_ref, h2_ref,
                  pbuf, ybuf, tail_ref, c_ref, m_ref, h_ref, s_ref, *, n, steps_per_seq, dm, dr, ds, n_main):
    step = pl.program_id(0)

    def reset_state():
        tail_ref[...] = jnp.zeros_like(tail_ref)
        c_ref[...] = jnp.zeros_like(c_ref)
        m_ref[...] = jnp.zeros_like(m_ref)
        h_ref[...] = jnp.zeros_like(h_ref)
        s_ref[...] = jnp.zeros_like(s_ref)

    @pl.when(step == 0)
    def _first():
        pbuf[1] = jnp.zeros(pbuf.shape[1:], pbuf.dtype)
        ybuf[...] = jnp.zeros_like(ybuf)
        reset_state()

    for half in range(2):
        if half == 1:
            pl.when(lax.rem(2 * step, steps_per_seq) == 0)(reset_state)
        blk = slice(half * n, (half + 1) * n)
        _front_block(xn_ref.at[blk], xres_ref, gin_ref, win_ref, cw_ref, cb_ref, wqk_ref, wv_ref, gb_ref,
                     gnm_ref, wa_ref, wx_ref, ba_ref, bx_ref, lam_ref, alog_ref, dskip_ref, gns_ref, wo_ref,
                     gpost_ref, gffn_ref, x1_ref, h2_ref, pbuf.at[half], pbuf.at[1 - half], ybuf,
                     ybuf.at[slice((1 - half) * n, (2 - half) * n)],
                     tail_ref, c_ref, m_ref, h_ref, s_ref, n=n, dm=dm, dr=dr, ds=ds, n_main=n_main,
                     out_project=(half == 1))


def _front_block(xn_ref, xres_ref, gin_ref, win_ref, cw_ref, cb_ref, wqk_ref, wv_ref, gb_ref, gnm_ref,
                 wa_ref, wx_ref, ba_ref, bx_ref, lam_ref, alog_ref, dskip_ref, gns_ref,
                 wo_ref, gpost_ref, gffn_ref, x1_ref, h2_ref, p_out, p_ref,
                 y_all, ybuf, tail_ref, c_ref, m_ref, h_ref, s_ref, *, n, dm, dr, ds, n_main, out_project):
    heads_m = MLSTM_HEADS
    dh = dm // heads_m
    hd = SSD_HEAD_DIM
    pair = 2 * hd
    n_pairs = ds // pair
    pairs_per_group = n_pairs // SSD_GROUPS
    n_all = win_ref.shape[1]
    n_conv = dm + dr + 2 * ds
    o0 = n_conv
    ns = SSD_STATE

    if out_project:
        mix = jnp.dot(y_all[...], wo_ref[...], preferred_element_type=F32)

    h_next = _rms(xn_ref[...], gin_ref[...]).astype(BF16)
    starts = list(range(0, n_all, FRONT_COLS))
    chunks = n // PERM
    per_call = -(-len(starts) // (8 * chunks))

    def project(k=per_call):
        for _ in range(k):
            if starts:
                c0 = starts.pop(0)
                c1 = min(c0 + FRONT_COLS, n_all)
                p_out[:, c0:c1] = jnp.dot(h_next, win_ref[:, c0:c1], preferred_element_type=F32)

    causal = _causal_mask(PERM)
    tri = jnp.where(causal, 1.0, 0.0).astype(BF16)
    lane = lax.broadcasted_iota(jnp.int32, (1, LANES), 1)

    for ch in range(chunks):
        rows = slice(ch * PERM, (ch + 1) * PERM)

        g = p_ref[rows, n_main:] + gb_ref[...]
        sp = _softplus(jnp.where(lane < GATE_DT, -g, g))
        dt = sp
        steps = jnp.where(lane < GATE_DT, -sp, sp * (-jnp.exp(alog_ref[...])))
        cs = _time_cumsum(tri, steps)
        project()

        nt_rows = tail_ref.shape[0]

        def conv_group(c0, c1):
            x_g = p_ref[rows, c0:c1]
            out = _perm_conv(x_g, tail_ref[:, c0:c1], cw_ref[:, c0:c1], cb_ref[:, c0:c1])
            tail_ref[:, c0:c1] = x_g[PERM - nt_rows:]
            return x_g, out

        x_m, conv_m = conv_group(0, dm)
        xc_m = _silu(conv_m)
        xm_bf = x_m.astype(BF16)
        qkv = [_mlstm_project(xc_m, xm_bf, wqk_ref, wv_ref, h, n=PERM, dh=dh) for h in range(heads_m)]
        cs_t, g_t = cs.T, g.T
        project()

        if ch == 0 and out_project:
            x1 = xres_ref[...] + _rms(mix, gpost_ref[...])
            x1_ref[...] = x1
            h2_ref[...] = _rms(x1, gffn_ref[...]).astype(h2_ref.dtype)
        project()

        _, conv_s = conv_group(dm + dr, n_conv)
        xs_bc = _silu(conv_s)
        b_g = [xs_bc[:, ds + grp * ns:ds + (grp + 1) * ns].astype(BF16) for grp in range(SSD_GROUPS)]
        c_g = [xs_bc[:, ds + (SSD_GROUPS + grp) * ns:ds + (SSD_GROUPS + grp + 1) * ns].astype(BF16)
               for grp in range(SSD_GROUPS)]
        cbm = [lax.dot_general(c_g[grp], b_g[grp], (((1,), (1,)), ((), ())), preferred_element_type=F32)
               for grp in range(SSD_GROUPS)]
        project()

        _, xc_r = conv_group(dm, dm + dr)
        xr_bf = xc_r.astype(BF16)
        tile = wa_ref.shape[1]
        ra = [jnp.dot(xr_bf[:, j * tile:(j + 1) * tile], wa_ref[j], preferred_element_type=F32)
              for j in range(dr // tile)]
        ri = [jnp.dot(xr_bf[:, j * tile:(j + 1) * tile], wx_ref[j], preferred_element_type=F32)
              for j in range(dr // tile)]
        r = _sigmoid(jnp.concatenate(ra, axis=1) + ba_ref[...])
        i = _sigmoid(jnp.concatenate(ri, axis=1) + bx_ref[...])
        a = jnp.exp((-RGLRU_C) * r * _softplus(-lam_ref[...]))
        w = 1.0 - a * a
        u = jnp.where(w > 0.0, w * lax.rsqrt(w), 0.0) * (i * xc_r)
        project()
        hseq, h_ref[...] = _rglru_scan(a, u, h_ref[...])
        yr = p_ref[rows, o0 + dm:o0 + dm + dr]
        ybuf[rows, dm:dm + dr] = (hseq * jax.nn.gelu(yr, approximate=True)).astype(ybuf.dtype)
        project()

        dec_m = [_mlstm_decay(g, cs, cs_t, g_t, causal, m_ref[h:h + 1, 0:1], h, n=PERM) for h in range(heads_m)]
        dec_s = [_ssd_decay(dt, cs, cs_t, causal, 2 * p, n=PERM, hd=hd) for p in range(n_pairs)]
        gw = pairs_per_group * pair
        yoff = [jnp.dot(c_g[grp], s_ref[:, grp * gw:(grp + 1) * gw].astype(BF16), preferred_element_type=F32)
                for grp in range(SSD_GROUPS)]
        ys, xds = [], []
        for j in range(max(heads_m, n_pairs)):
            if j < heads_m:
                sl = slice(j * dh, (j + 1) * dh)
                hh, c_ref[j] = _mlstm_output(*qkv[j], dec_m[j], c_ref[j], dh=dh)
                m_ref[j:j + 1, :] = jnp.broadcast_to(dec_m[j]["m_new"], (1, LANES))
                o = _sigmoid(p_ref[rows, o0 + j * dh:o0 + (j + 1) * dh]) * hh
                ybuf[rows, sl] = _rms(o, gnm_ref[:, sl]).astype(ybuf.dtype)
            if j % 2 == 1:
                project()
            if j < n_pairs:
                ps = slice(j * pair, (j + 1) * pair)
                grp, jg = divmod(j, pairs_per_group)
                y_p, xd_p = _ssd_output(xs_bc[:, ps], cbm[grp], dec_s[j], yoff[grp][:, jg * pair:(jg + 1) * pair],
                                        dskip_ref[:, ps], n=PERM, hd=hd)
                ys.append(y_p)
                xds.append(xd_p)
                if jg == pairs_per_group - 1:
                    gs = slice(grp * gw, (grp + 1) * gw)
                    s_loc = lax.dot_general(b_g[grp], jnp.concatenate(xds[-pairs_per_group:], axis=1),
                                            (((0,), (0,)), ((), ())), preferred_element_type=F32)
                    e_tot = jnp.concatenate([dec_s[p]["e_tot"] for p in range(j + 1 - pairs_per_group, j + 1)],
                                            axis=1)
                    s_ref[:, gs] = e_tot * s_ref[:, gs] + s_loc
        project()
        z = p_ref[rows, o0 + dm + dr:o0 + dm + dr + ds]
        ybuf[rows, dm + dr:] = _rms(jnp.concatenate(ys, axis=1) * _silu(z), gns_ref[...]).astype(ybuf.dtype)
    project(len(starts))


def _front(x2, gin, win, cw, cb, wqk, wv, gb, gnm, wa, wx, ba, bx, lam, alog, dskip, gns, wo, gpost, gffn,
           *, batch, seq, n_main):
    n = ROWS_FRONT
    t, d = x2.shape
    dm, dr, ds = gnm.shape[1], lam.shape[1], gns.shape[1]
    nt = seq // n
    nblk = batch * nt
    k = cw.shape[0]
    consts = [gin, win, cw, cb, wqk, wv, gb, gnm, wa, wx, ba, bx, lam, alog, dskip, gns, wo, gpost, gffn]
    const_spec = lambda a: pl.BlockSpec(a.shape, lambda s, _nd=a.ndim: (0,) * _nd)
    nstep = nblk // 2
    nxt = lambda s: (jnp.minimum(s, nstep - 1), 0)
    done = lambda s: (jnp.clip(s - 1, 0, nstep - 1), 0)
    return pl.pallas_call(
        functools.partial(_front_kernel, n=n, steps_per_seq=nt, dm=dm, dr=dr, ds=ds, n_main=n_main),
        grid=(nstep + 1,),
        in_specs=[pl.BlockSpec((2 * n, d), nxt), pl.BlockSpec((2 * n, d), done)] + [const_spec(a) for a in consts],
        out_specs=[pl.BlockSpec((2 * n, d), done), pl.BlockSpec((2 * n, d), done)],
        out_shape=[jax.ShapeDtypeStruct((t, d), F32), jax.ShapeDtypeStruct((t, d), BF16)],
        scratch_shapes=[
            pltpu.VMEM((2, n, win.shape[1]), F32),
            pltpu.VMEM((2 * n, dm + dr + ds), BF16),
            pltpu.VMEM((SUBLANES * (k - 1), cw.shape[1]), F32),
            pltpu.VMEM((MLSTM_HEADS, dm // MLSTM_HEADS, 2 * dm // MLSTM_HEADS), F32),
            pltpu.VMEM((SUBLANES, LANES), F32),
            pltpu.VMEM((1, dr), F32),
            pltpu.VMEM((SSD_STATE, ds), F32),
        ],
        compiler_params=pltpu.CompilerParams(
            dimension_semantics=("arbitrary",), vmem_limit_bytes=VMEM_LIMIT),
        name="front",
    )(x2, x2, *consts)


def _ffn_kernel(h_ref, hh_ref, x_ref, wup_ref, cw_ref, cb_ref, wdn_ref, gpost_ref, o_ref, he_ref, acc_ref,
                *, d_ff, cols, sub):
    t = pl.program_id(1)
    tm = h_ref.shape[0]
    halo = FFN_HALO
    he_ref[0:halo, :] = jnp.where(t > 0, hh_ref[...], jnp.zeros_like(hh_ref))
    he_ref[halo:, :] = h_ref[...]

    def up(item):
        r0, c0 = item
        he = he_ref[r0:r0 + halo + sub, :]
        return [jnp.dot(he, wup_ref[:, base:base + cols], preferred_element_type=F32)
                for base in (c0, d_ff + c0)]

    def gate(item, us):
        _, c0 = item
        branch = []
        for base, u in zip((c0, d_ff + c0), us):
            cs = slice(base, base + cols)
            w = cw_ref[:, cs]
            b = cb_ref[:, cs]
            parts = []
            for p0 in range(0, sub, PERM):
                blk = u[halo + p0:halo + p0 + PERM]
                parts.append(_perm_conv(blk, u[p0:p0 + halo], w, b))
            branch.append(jnp.concatenate(parts, axis=0))
        return (jax.nn.gelu(branch[0], approximate=True) * branch[1]).astype(BF16)

    items = [(r0, c0) for r0 in range(0, tm, sub) for c0 in range(0, d_ff, cols)]
    us = up(items[0])
    for i, item in enumerate(items):
        us_next = up(items[i + 1]) if i + 1 < len(items) else None
        r0, c0 = item
        rs = slice(r0, r0 + sub)
        down = jnp.dot(gate(item, us), wdn_ref[c0:c0 + cols, :], preferred_element_type=F32)
        if c0 == 0:
            acc_ref[rs, :] = down
        else:
            acc_ref[rs, :] += down
        if c0 + cols >= d_ff:
            o_ref[rs, :] = x_ref[rs, :] + _rms(acc_ref[rs, :], gpost_ref[...])
        us = us_next


def _ffn(h2, x1, wup, cw, cb, wdn, gpost, *, batch, seq):
    t, d = x1.shape
    d_ff = wdn.shape[0]
    tm = ROWS_FFN
    nt = seq // tm
    assert FFN_HALO == SUBLANES * (cw.shape[0] - 1)
    rows = lambda b, i: (b * nt + i, 0)
    halo = lambda b, i: (jnp.maximum((b * seq + i * tm) // FFN_HALO - 1, 0), 0)
    const = lambda b, i: (0, 0)
    return pl.pallas_call(
        functools.partial(_ffn_kernel, d_ff=d_ff, cols=FFN_COLS, sub=FFN_SUB),
        grid=(batch, nt),
        in_specs=[
            pl.BlockSpec((tm, d), rows),
            pl.BlockSpec((FFN_HALO, d), halo),
            pl.BlockSpec((tm, d), rows),
            pl.BlockSpec(wup.shape, const),
            pl.BlockSpec(cw.shape, const),
            pl.BlockSpec(cb.shape, const),
            pl.BlockSpec(wdn.shape, const),
            pl.BlockSpec((1, d), const),
        ],
        out_specs=pl.BlockSpec((tm, d), rows),
        out_shape=jax.ShapeDtypeStruct((t, d), F32),
        scratch_shapes=[pltpu.VMEM((FFN_HALO + tm, d), BF16), pltpu.VMEM((tm, d), F32)],
        compiler_params=pltpu.CompilerParams(
            dimension_semantics=("arbitrary", "arbitrary"), vmem_limit_bytes=VMEM_LIMIT),
        name="ffn",
    )(h2, h2, x1, wup, cw, cb, wdn, gpost)


def _cast_kernel(w_ref, o_ref):
    o_ref[...] = w_ref[...].astype(o_ref.dtype)


def _layer_bf16(w, layer):
    _, r, c = w.shape
    tr = ROWS_CAST
    return pl.pallas_call(
        _cast_kernel,
        grid=(r // tr,),
        in_specs=[pl.BlockSpec((None, tr, c), lambda i: (layer, i, 0))],
        out_specs=pl.BlockSpec((tr, c), lambda i: (i, 0)),
        out_shape=jax.ShapeDtypeStruct((r, c), BF16),
        compiler_params=pltpu.CompilerParams(
            dimension_semantics=("arbitrary",), vmem_limit_bytes=VMEM_LIMIT),
        name="cast",
    )(w)


def _regroup_kernel(w_ref, o_ref, *, groups):
    c = 0
    for a, b in groups:
        o_ref[:, c:c + b - a] = w_ref[:, a:b].astype(o_ref.dtype)
        c += b - a
    o_ref[:, c:] = jnp.zeros((o_ref.shape[0], o_ref.shape[1] - c), o_ref.dtype)


def _regrouped_bf16(w, layer, groups, n_out):
    _, r, c = w.shape
    tr = ROWS_CAST
    return pl.pallas_call(
        functools.partial(_regroup_kernel, groups=groups),
        grid=(r // tr,),
        in_specs=[pl.BlockSpec((None, tr, c), lambda i: (layer, i, 0))],
        out_specs=pl.BlockSpec((tr, n_out), lambda i: (i, 0)),
        out_shape=jax.ShapeDtypeStruct((r, n_out), BF16),
        compiler_params=pltpu.CompilerParams(
            dimension_semantics=("arbitrary",), vmem_limit_bytes=VMEM_LIMIT),
        name="regroup",
    )(w)


def _block_diag(w, per_tile):
    nb, e, _ = w.shape
    w = w.reshape(nb // per_tile, per_tile, e, e)
    eye = jnp.eye(per_tile, dtype=w.dtype)
    out = jnp.einsum("tpij,pq->tpiqj", w, eye)
    return out.reshape(nb // per_tile, per_tile * e, per_tile * e)


def _to_segment_order(x):
    b, s, d = x.shape
    x = x.reshape(b, s // PERM, SUBLANES, PERM // SUBLANES, d)
    return jnp.swapaxes(x, 2, 3).reshape(b * s, d)


def _from_segment_order(x2, b, s):
    d = x2.shape[1]
    x = x2.reshape(b, s // PERM, PERM // SUBLANES, SUBLANES, d)
    return jnp.swapaxes(x, 2, 3).reshape(b, s, d)


def kernel(x, norm_mix_pre, norm_mix_post, norm_ffn_pre, norm_ffn_post, w_in, conv_m_w, conv_m_b, w_q_m, w_k_m, w_v_m, b_i_m, b_f_m, norm_m, conv_r_w, conv_r_b, w_a_r, b_a_r, w_x_r, b_x_r, lam_r, conv_s_w, conv_s_b, dt_bias_s, a_log_s, d_skip_s, norm_s, w_out, w_up, conv_f_w, conv_f_b, w_down):
    batch, seq, d = x.shape
    depth = w_in.shape[0]
    dm = conv_m_w.shape[2]
    dr = conv_r_w.shape[2]
    ds = norm_s.shape[1]
    dconv = conv_s_w.shape[2]
    hm = b_i_m.shape[1]
    hs = dt_bias_s.shape[1]
    sizes = (dm, dm, hm, hm, dr, dr, ds, dconv, hs)
    offs = [0]
    for s in sizes:
        offs.append(offs[-1] + s)
    col = lambda j: (offs[j], offs[j + 1])
    main_cols = [col(0), col(4), col(7), col(1), col(5), col(6)]
    n_main = sum(b - a for a, b in main_cols)

    row = lambda v: v.reshape(1, -1).astype(F32)
    x2 = _to_segment_order(x)
    for l in range(depth):
        w_all = _regrouped_bf16(w_in, l, tuple(main_cols) + ((offs[2], offs[4]), (offs[8], offs[9])),
                                n_main + LANES)
        gate_b = jnp.concatenate([b_i_m[l], b_f_m[l], dt_bias_s[l],
                                  jnp.zeros((LANES - 2 * hm - hs,), F32)]).reshape(1, LANES)
        alog = jnp.concatenate([jnp.zeros((GATE_DT,), F32), a_log_s[l],
                                jnp.zeros((LANES - GATE_DT - hs,), F32)]).reshape(1, LANES)
        cw = jnp.concatenate([conv_m_w[l], conv_r_w[l], conv_s_w[l]], axis=1)
        cb = jnp.concatenate([conv_m_b[l], conv_r_b[l], conv_s_b[l]]).reshape(1, -1)

        x1, h2 = _front(x2, row(norm_mix_pre[l]), w_all, cw, cb, jnp.concatenate([w_q_m[l], w_k_m[l]], axis=-1).astype(BF16),
                        w_v_m[l].astype(BF16), gate_b, row(norm_m[l]), _block_diag(w_a_r[l], 4).astype(BF16),
                        _block_diag(w_x_r[l], 4).astype(BF16), row(b_a_r[l]), row(b_x_r[l]), row(lam_r[l]),
                        alog, row(jnp.repeat(d_skip_s[l], SSD_HEAD_DIM)), row(norm_s[l]),
                        _layer_bf16(w_out, l), row(norm_mix_post[l]), row(norm_ffn_pre[l]),
                        batch=batch, seq=seq, n_main=n_main)
        x2 = _ffn(h2, x1, _layer_bf16(w_up, l), conv_f_w[l], row(conv_f_b[l]),
                  _layer_bf16(w_down, l), row(norm_ffn_post[l]), batch=batch, seq=seq)
    return _from_segment_order(x2, batch, seq)
```

```python
import functools

import jax
import jax.numpy as jnp
from jax import lax
from jax.experimental import pallas as pl
from jax.experimental.pallas import tpu as pltpu

F32 = jnp.float32
BF16 = jnp.bfloat16
EPS = 1e-6

LANES = 128
SUBLANES = 8
VMEM_LIMIT = 56 * 1024 * 1024

MLSTM_HEADS = 4
RGLRU_C = 8.0
SSD_HEAD_DIM = 64
SSD_GROUPS = 2
SSD_STATE = 128
GATE_I, GATE_F, GATE_DT = 0, 4, 8

PERM = 128
ROWS_FRONT = 256
FRONT_COLS = 256
ROWS_FFN = 1024
FFN_SUB = 256
ROWS_CAST = 256
FFN_COLS = 256
FFN_HALO = 16


def _rms(x, g):
    return x * lax.rsqrt(jnp.mean(x * x, axis=-1, keepdims=True) + EPS) * g


def _softplus(x):
    return jnp.maximum(x, 0.0) + jnp.log(1.0 + jnp.exp(-jnp.abs(x)))


def _sigmoid(x):
    return 0.5 * jnp.tanh(0.5 * x) + 0.5


def _silu(x):
    hx = 0.5 * x
    return hx * jnp.tanh(hx) + hx


def _wrap_tail(prev_tail, tail):
    out = []
    for j in range(tail.shape[0] // SUBLANES):
        rs = slice(j * SUBLANES, (j + 1) * SUBLANES)
        sub = lax.broadcasted_iota(jnp.int32, (SUBLANES, tail.shape[1]), 0)
        mixed = jnp.where(sub == SUBLANES - 1, prev_tail[rs], tail[rs])
        out.append(pltpu.roll(mixed, 1, 0))
    return jnp.concatenate(out, axis=0)


def _perm_conv(x, prev_tail, w, b):
    n = x.shape[0]
    k = w.shape[0]
    nt = SUBLANES * (k - 1)
    ext = jnp.concatenate([_wrap_tail(prev_tail, x[n - nt:]), x], axis=0)
    out = b + w[k - 1:k] * x
    for j in range(1, k):
        start = nt - SUBLANES * j
        out = out + w[k - 1 - j:k - j] * ext[start:start + n]
    return out


def _causal_mask(n):
    seg = n // SUBLANES
    row = lax.broadcasted_iota(jnp.int32, (n, n), 0)
    col = lax.broadcasted_iota(jnp.int32, (n, n), 1)
    t_row = (row & (SUBLANES - 1)) * seg + (row >> 3)
    t_col = (col & (SUBLANES - 1)) * seg + (col >> 3)
    return t_row >= t_col


def _time_cumsum(tri_bf, steps):
    c = steps.shape[1]
    hi = steps.astype(BF16)
    rest = steps - hi.astype(F32)
    mid = rest.astype(BF16)
    lo = (rest - mid.astype(F32)).astype(BF16)
    top = jnp.dot(tri_bf, jnp.concatenate([hi, mid], axis=1), preferred_element_type=F32)
    return top[:, :c] + (top[:, c:] + jnp.dot(tri_bf, lo, preferred_element_type=F32))


def _lane_pair(col_lo, col_hi, n, width):
    lane = lax.broadcasted_iota(jnp.int32, (n, width), 1)
    return jnp.where(lane < width // 2, col_lo, col_hi)


def _mlstm_project(xc, xm_bf, wqk_ref, wv_ref, h, *, n, dh):
    sl = slice(h * dh, (h + 1) * dh)
    qk = jnp.dot(xc[:, sl].astype(BF16), wqk_ref[h], preferred_element_type=F32)
    v = jnp.dot(xm_bf[:, sl], wv_ref[h], preferred_element_type=F32).astype(BF16)
    return qk[:, :dh].astype(BF16), qk[:, dh:] * (dh ** -0.5), jnp.concatenate([v, jnp.ones((n, dh), BF16)], axis=1)


def _mlstm_decay(g, cs, cs_t, g_t, causal, m_prev, h, *, n):
    bcol = cs[:, GATE_F + h:GATE_F + h + 1]
    brow = cs_t[GATE_F + h:GATE_F + h + 1, :]
    icol = g[:, GATE_I + h:GATE_I + h + 1]
    irow = g_t[GATE_I + h:GATE_I + h + 1, :]
    btot = cs[n - 1:n, GATE_F + h:GATE_F + h + 1]
    d = jnp.where(causal, bcol - brow + irow, -jnp.inf)
    inter = bcol + m_prev
    m_t = jnp.maximum(inter, jnp.max(d, axis=1, keepdims=True))
    wst = btot - bcol + icol
    m_loc = jnp.max(wst, axis=0, keepdims=True)
    m_new = jnp.maximum(btot + m_prev, m_loc)
    return dict(pmat=jnp.exp(d - m_t), e_inter=jnp.exp(inter - m_t), floor=jnp.exp(-m_t),
                ew=jnp.exp(wst - m_loc), s_prev=jnp.exp(btot + m_prev - m_new),
                s_loc=jnp.exp(m_loc - m_new), m_new=m_new)


def _mlstm_output(q, k, v_aug, dec, c_prev, *, dh):
    scores = lax.dot_general(q, k.astype(BF16), (((1,), (1,)), ((), ())),
                             preferred_element_type=F32) * dec["pmat"]
    comb = (jnp.dot(scores.astype(BF16), v_aug, preferred_element_type=F32)
            + dec["e_inter"] * jnp.dot(q, c_prev.astype(BF16), preferred_element_type=F32))
    hh = comb[:, :dh] / jnp.maximum(jnp.abs(comb[:, dh:]), dec["floor"])
    ek = (dec["ew"] * k).astype(BF16)
    c_loc = lax.dot_general(ek, v_aug, (((0,), (0,)), ((), ())), preferred_element_type=F32)
    return hh, dec["s_prev"] * c_prev + dec["s_loc"] * c_loc


def _rglru_scan(a, u, carry):
    n, d = a.shape
    tiles = n // SUBLANES
    hs, ps = [u[0:SUBLANES]], [a[0:SUBLANES]]
    for i in range(1, tiles):
        rs = slice(i * SUBLANES, (i + 1) * SUBLANES)
        hs.append(a[rs] * hs[-1] + u[rs])
        ps.append(a[rs] * ps[-1])
    gacc, pacc = hs[-1], ps[-1]
    sub = lax.broadcasted_iota(jnp.int32, (SUBLANES, d), 0)
    for s in (1, 2, 4):
        keep = sub >= s
        gacc = jnp.where(keep, pacc * pltpu.roll(gacc, s, 0) + gacc, gacc)
        pacc = jnp.where(keep, pacc * pltpu.roll(pacc, s, 0), pacc)
    seg_end = gacc + pacc * carry
    seg_in = jnp.where(sub == 0, carry, pltpu.roll(seg_end, 1, 0))
    out = [hs[i] + ps[i] * seg_in for i in range(tiles)]
    return jnp.concatenate(out, axis=0), seg_end[SUBLANES - 1:SUBLANES]


def _ssd_decay(dt, acs, acs_t, causal, h0, *, n, hd):
    pair = 2 * hd
    cols = [GATE_DT + h0, GATE_DT + h0 + 1]
    acol = [acs[:, c:c + 1] for c in cols]
    arow = [acs_t[c:c + 1, :] for c in cols]
    atot = [acs[n - 1:n, c:c + 1] for c in cols]
    return dict(
        dec=[jnp.exp(jnp.where(causal, acol[e] - arow[e], -jnp.inf)) for e in range(2)],
        dt=_lane_pair(dt[:, cols[0]:cols[0] + 1], dt[:, cols[1]:cols[1] + 1], n, pair),
        e_in=jnp.exp(_lane_pair(acol[0], acol[1], n, pair)),
        e_out=jnp.exp(_lane_pair(atot[0] - acol[0], atot[1] - acol[1], n, pair)),
        e_tot=jnp.exp(_lane_pair(atot[0], atot[1], 1, pair)))


def _ssd_output(xs_p, cbm, dec, yoff_p, dskip_p, *, n, hd):
    pair = 2 * hd
    lo = lax.broadcasted_iota(jnp.int32, (n, pair), 1) < hd
    xdt = xs_p * dec["dt"]
    ydiag = jnp.zeros((n, pair), F32)
    for e in range(2):
        half = jnp.where(lo if e == 0 else jnp.logical_not(lo), xdt, 0.0).astype(BF16)
        ydiag = ydiag + jnp.dot((cbm * dec["dec"][e]).astype(BF16), half, preferred_element_type=F32)
    return ydiag + yoff_p * dec["e_in"] + xs_p * dskip_p, (xdt * dec["e_out"]).astype(BF16)


def _front_kernel(xn_ref, xres_ref, gin_ref, win_ref, cw_ref, cb_ref, wqk_ref, wv_ref, gb_ref, gnm_ref,
                  wa_ref, wx_ref, ba_ref, bx_ref, lam_ref, alog_ref, dskip_ref, gns_ref,
                  wo_ref, gpost_ref, gffn_ref, x1_ref, h2_ref,
                  pbuf, ybuf, tail_ref, c_ref, m_ref, h_ref, s_ref, *, n, steps_per_seq, dm, dr, ds, n_main):
    step = pl.program_id(0)

    def reset_state():
        tail_ref[...] = jnp.zeros_like(tail_ref)
        c_ref[...] = jnp.zeros_like(c_ref)
        m_ref[...] = jnp.zeros_like(m_ref)
        h_ref[...] = jnp.zeros_like(h_ref)
        s_ref[...] = jnp.zeros_like(s_ref)

    @pl.when(step == 0)
    def _first():
        pbuf[1] = jnp.zeros(pbuf.shape[1:], pbuf.dtype)
        ybuf[...] = jnp.zeros_like(ybuf)
        reset_state()

    for half in range(2):
        if half == 1:
            pl.when(lax.rem(2 * step, steps_per_seq) == 0)(reset_state)
        blk = slice(half * n, (half + 1) * n)
        _front_block(xn_ref.at[blk], xres_ref, gin_ref, win_ref, cw_ref, cb_ref, wqk_ref, wv_ref, gb_ref,
                     gnm_ref, wa_ref, wx_ref, ba_ref, bx_ref, lam_ref, alog_ref, dskip_ref, gns_ref, wo_ref,
                     gpost_ref, gffn_ref, x1_ref, h2_ref, pbuf.at[half], pbuf.at[1 - half], ybuf,
                     ybuf.at[slice((1 - half) * n, (2 - half) * n)],
                     tail_ref, c_ref, m_ref, h_ref, s_ref, n=n, dm=dm, dr=dr, ds=ds, n_main=n_main,
                     out_project=(half == 1))


def _front_block(xn_ref, xres_ref, gin_ref, win_ref, cw_ref, cb_ref, wqk_ref, wv_ref, gb_ref, gnm_ref,
                 wa_ref, wx_ref, ba_ref, bx_ref, lam_ref, alog_ref, dskip_ref, gns_ref,
                 wo_ref, gpost_ref, gffn_ref, x1_ref, h2_ref, p_out, p_ref,
                 y_all, ybuf, tail_ref, c_ref, m_ref, h_ref, s_ref, *, n, dm, dr, ds, n_main, out_project):
    heads_m = MLSTM_HEADS
    dh = dm // heads_m
    hd = SSD_HEAD_DIM
    pair = 2 * hd
    n_pairs = ds // pair
    pairs_per_group = n_pairs // SSD_GROUPS
    n_all = win_ref.shape[1]
    n_conv = dm + dr + 2 * ds
    o0 = n_conv
    ns = SSD_STATE

    if out_project:
        mix = jnp.dot(y_all[...], wo_ref[...], preferred_element_type=F32)

    h_next = _rms(xn_ref[...], gin_ref[...]).astype(BF16)
    starts = list(range(0, n_all, FRONT_COLS))
    chunks = n // PERM
    per_call = -(-len(starts) // (8 * chunks))

    def project(k=per_call):
        for _ in range(k):
            if starts:
                c0 = starts.pop(0)
                c1 = min(c0 + FRONT_COLS, n_all)
                p_out[:, c0:c1] = jnp.dot(h_next, win_ref[:, c0:c1], preferred_element_type=F32)

    causal = _causal_mask(PERM)
    tri = jnp.where(causal, 1.0, 0.0).astype(BF16)
    lane = lax.broadcasted_iota(jnp.int32, (1, LANES), 1)

    for ch in range(chunks):
        rows = slice(ch * PERM, (ch + 1) * PERM)

        g = p_ref[rows, n_main:] + gb_ref[...]
        sp = _softplus(jnp.where(lane < GATE_DT, -g, g))
        dt = sp
        steps = jnp.where(lane < GATE_DT, -sp, sp * (-jnp.exp(alog_ref[...])))
        cs = _time_cumsum(tri, steps)
        project()

        nt_rows = tail_ref.shape[0]

        def conv_group(c0, c1):
            x_g = p_ref[rows, c0:c1]
            out = _perm_conv(x_g, tail_ref[:, c0:c1], cw_ref[:, c0:c1], cb_ref[:, c0:c1])
            tail_ref[:, c0:c1] = x_g[PERM - nt_rows:]
            return x_g, out

        x_m, conv_m = conv_group(0, dm)
        xc_m = _silu(conv_m)
        xm_bf = x_m.astype(BF16)
        qkv = [_mlstm_project(xc_m, xm_bf, wqk_ref, wv_ref, h, n=PERM, dh=dh) for h in range(heads_m)]
        cs_t, g_t = cs.T, g.T
        project()

        if ch == 0 and out_project:
            x1 = xres_ref[...] + _rms(mix, gpost_ref[...])
            x1_ref[...] = x1
            h2_ref[...] = _rms(x1, gffn_ref[...]).astype(h2_ref.dtype)
        project()

        _, conv_s = conv_group(dm + dr, n_conv)
        xs_bc = _silu(conv_s)
        b_g = [xs_bc[:, ds + grp * ns:ds + (grp + 1) * ns].astype(BF16) for grp in range(SSD_GROUPS)]
        c_g = [xs_bc[:, ds + (SSD_GROUPS + grp) * ns:ds + (SSD_GROUPS + grp + 1) * ns].astype(BF16)
               for grp in range(SSD_GROUPS)]
        cbm = [lax.dot_general(c_g[grp], b_g[grp], (((1,), (1,)), ((), ())), preferred_element_type=F32)
               for grp in range(SSD_GROUPS)]
        project()

        _, xc_r = conv_group(dm, dm + dr)
        xr_bf = xc_r.astype(BF16)
        tile = wa_ref.shape[1]
        ra = [jnp.dot(xr_bf[:, j * tile:(j + 1) * tile], wa_ref[j], preferred_element_type=F32)
              for j in range(dr // tile)]
        ri = [jnp.dot(xr_bf[:, j * tile:(j + 1) * tile], wx_ref[j], preferred_element_type=F32)
              for j in range(dr // tile)]
        r = _sigmoid(jnp.concatenate(ra, axis=1) + ba_ref[...])
        i = _sigmoid(jnp.concatenate(ri, axis=1) + bx_ref[...])
        a = jnp.exp((-RGLRU_C) * r * _softplus(-lam_ref[...]))
        w = 1.0 - a * a
        u = jnp.where(w > 0.0, w * lax.rsqrt(w), 0.0) * (i * xc_r)
        project()
        hseq, h_ref[...] = _rglru_scan(a, u, h_ref[...])
        yr = p_ref[rows, o0 + dm:o0 + dm + dr]
        ybuf[rows, dm:dm + dr] = (hseq * jax.nn.gelu(yr, approximate=True)).astype(ybuf.dtype)
        project()

        dec_m = [_mlstm_decay(g, cs, cs_t, g_t, causal, m_ref[h:h + 1, 0:1], h, n=PERM) for h in range(heads_m)]
        dec_s = [_ssd_decay(dt, cs, cs_t, causal, 2 * p, n=PERM, hd=hd) for p in range(n_pairs)]
        gw = pairs_per_group * pair
        yoff = [jnp.dot(c_g[grp], s_ref[:, grp * gw:(grp + 1) * gw].astype(BF16), preferred_element_type=F32)
                for grp in range(SSD_GROUPS)]
        ys, xds = [], []
        for j in range(max(heads_m, n_pairs)):
            if j < heads_m:
                sl = slice(j * dh, (j + 1) * dh)
                hh, c_ref[j] = _mlstm_output(*qkv[j], dec_m[j], c_ref[j], dh=dh)
                m_ref[j:j + 1, :] = jnp.broadcast_to(dec_m[j]["m_new"], (1, LANES))
                o = _sigmoid(p_ref[rows, o0 + j * dh:o0 + (j + 1) * dh]) * hh
                ybuf[rows, sl] = _rms(o, gnm_ref[:, sl]).astype(ybuf.dtype)
            if j % 2 == 1:
                project()
            if j < n_pairs:
                ps = slice(j * pair, (j + 1) * pair)
                grp, jg = divmod(j, pairs_per_group)
                y_p, xd_p = _ssd_output(xs_bc[:, ps], cbm[grp], dec_s[j], yoff[grp][:, jg * pair:(jg + 1) * pair],
                                        dskip_ref[:, ps], n=PERM, hd=hd)
                ys.append(y_p)
                xds.append(xd_p)
                if jg == pairs_per_group - 1:
                    gs = slice(grp * gw, (grp + 1) * gw)
                    s_loc = lax.dot_general(b_g[grp], jnp.concatenate(xds[-pairs_per_group:], axis=1),
                                            (((0,), (0,)), ((), ())), preferred_element_type=F32)
                    e_tot = jnp.concatenate([dec_s[p]["e_tot"] for p in range(j + 1 - pairs_per_group, j + 1)],
                                            axis=1)
                    s_ref[:, gs] = e_tot * s_ref[:, gs] + s_loc
        project()
        z = p_ref[rows, o0 + dm + dr:o0 + dm + dr + ds]
        ybuf[rows, dm + dr:] = _rms(jnp.concatenate(ys, axis=1) * _silu(z), gns_ref[...]).astype(ybuf.dtype)
    project(len(starts))


def _front(x2, gin, win, cw, cb, wqk, wv, gb, gnm, wa, wx, ba, bx, lam, alog, dskip, gns, wo, gpost, gffn,
           *, batch, seq, n_main):
    n = ROWS_FRONT
    t, d = x2.shape
    dm, dr, ds = gnm.shape[1], lam.shape[1], gns.shape[1]
    nt = seq // n
    nblk = batch * nt
    k = cw.shape[0]
    consts = [gin, win, cw, cb, wqk, wv, gb, gnm, wa, wx, ba, bx, lam, alog, dskip, gns, wo, gpost, gffn]
    const_spec = lambda a: pl.BlockSpec(a.shape, lambda s, _nd=a.ndim: (0,) * _nd)
    nstep = nblk // 2
    nxt = lambda s: (jnp.minimum(s, nstep - 1), 0)
    done = lambda s: (jnp.clip(s - 1, 0, nstep - 1), 0)
    return pl.pallas_call(
        functools.partial(_front_kernel, n=n, steps_per_seq=nt, dm=dm, dr=dr, ds=ds, n_main=n_main),
        grid=(nstep + 1,),
        in_specs=[pl.BlockSpec((2 * n, d), nxt), pl.BlockSpec((2 * n, d), done)] + [const_spec(a) for a in consts],
        out_specs=[pl.BlockSpec((2 * n, d), done), pl.BlockSpec((2 * n, d), done)],
        out_shape=[jax.ShapeDtypeStruct((t, d), F32), jax.ShapeDtypeStruct((t, d), BF16)],
        scratch_shapes=[
            pltpu.VMEM((2, n, win.shape[1]), F32),
            pltpu.VMEM((2 * n, dm + dr + ds), BF16),
            pltpu.VMEM((SUBLANES * (k - 1), cw.shape[1]), F32),
            pltpu.VMEM((MLSTM_HEADS, dm // MLSTM_HEADS, 2 * dm // MLSTM_HEADS), F32),
            pltpu.VMEM((SUBLANES, LANES), F32),
            pltpu.VMEM((1, dr), F32),
            pltpu.VMEM((SSD_STATE, ds), F32),
        ],
        compiler_params=pltpu.CompilerParams(
            dimension_semantics=("arbitrary",), vmem_limit_bytes=VMEM_LIMIT),
        name="front",
    )(x2, x2, *consts)


def _ffn_kernel(h_ref, hh_ref, x_ref, wup_ref, cw_ref, cb_ref, wdn_ref, gpost_ref, o_ref, he_ref, acc_ref,
                *, d_ff, cols, sub):
    t = pl.program_id(1)
    tm = h_ref.shape[0]
    halo = FFN_HALO
    he_ref[0:halo, :] = jnp.where(t > 0, hh_ref[...], jnp.zeros_like(hh_ref))
    he_ref[halo:, :] = h_ref[...]

    def up(item):
        r0, c0 = item
        he = he_ref[r0:r0 + halo + sub, :]
        return [jnp.dot(he, wup_ref[:, base:base + cols], preferred_element_type=F32)
                for base in (c0, d_ff + c0)]

    def gate(item, us):
        _, c0 = item
        branch = []
        for base, u in zip((c0, d_ff + c0), us):
            cs = slice(base, base + cols)
            w = cw_ref[:, cs]
            b = cb_ref[:, cs]
            parts = []
            for p0 in range(0, sub, PERM):
                blk = u[halo + p0:halo + p0 + PERM]
                parts.append(_perm_conv(blk, u[p0:p0 + halo], w, b))
            branch.append(jnp.concatenate(parts, axis=0))
        return (jax.nn.gelu(branch[0], approximate=True) * branch[1]).astype(BF16)

    items = [(r0, c0) for p0 in range(0, tm, 2 * sub) for c0 in range(0, d_ff, cols)
             for r0 in range(p0, min(p0 + 2 * sub, tm), sub)]
    us = up(items[0])
    for i, item in enumerate(items):
        us_next = up(items[i + 1]) if i + 1 < len(items) else None
        r0, c0 = item
        rs = slice(r0, r0 + sub)
        down = jnp.dot(gate(item, us), wdn_ref[c0:c0 + cols, :], preferred_element_type=F32)
        if c0 == 0:
            acc_ref[rs, :] = down
        else:
            acc_ref[rs, :] += down
        if c0 + cols >= d_ff:
            o_ref[rs, :] = x_ref[rs, :] + _rms(acc_ref[rs, :], gpost_ref[...])
        us = us_next


def _ffn(h2, x1, wup, cw, cb, wdn, gpost, *, batch, seq):
    t, d = x1.shape
    d_ff = wdn.shape[0]
    tm = ROWS_FFN
    nt = seq // tm
    assert FFN_HALO == SUBLANES * (cw.shape[0] - 1)
    rows = lambda b, i: (b * nt + i, 0)
    halo = lambda b, i: (jnp.maximum((b * seq + i * tm) // FFN_HALO - 1, 0), 0)
    const = lambda b, i: (0, 0)
    return pl.pallas_call(
        functools.partial(_ffn_kernel, d_ff=d_ff, cols=FFN_COLS, sub=FFN_SUB),
        grid=(batch, nt),
        in_specs=[
            pl.BlockSpec((tm, d), rows),
            pl.BlockSpec((FFN_HALO, d), halo),
            pl.BlockSpec((tm, d), rows),
            pl.BlockSpec(wup.shape, const),
            pl.BlockSpec(cw.shape, const),
            pl.BlockSpec(cb.shape, const),
            pl.BlockSpec(wdn.shape, const),
            pl.BlockSpec((1, d), const),
        ],
        out_specs=pl.BlockSpec((tm, d), rows),
        out_shape=jax.ShapeDtypeStruct((t, d), F32),
        scratch_shapes=[pltpu.VMEM((FFN_HALO + tm, d), BF16), pltpu.VMEM((tm, d), F32)],
        compiler_params=pltpu.CompilerParams(
            dimension_semantics=("arbitrary", "arbitrary"), vmem_limit_bytes=VMEM_LIMIT),
        name="ffn",
    )(h2, h2, x1, wup, cw, cb, wdn, gpost)


def _cast_kernel(w_ref, o_ref):
    o_ref[...] = w_ref[...].astype(o_ref.dtype)


def _layer_bf16(w, layer):
    _, r, c = w.shape
    tr = ROWS_CAST
    return pl.pallas_call(
        _cast_kernel,
        grid=(r // tr,),
        in_specs=[pl.BlockSpec((None, tr, c), lambda i: (layer, i, 0))],
        out_specs=pl.BlockSpec((tr, c), lambda i: (i, 0)),
        out_shape=jax.ShapeDtypeStruct((r, c), BF16),
        compiler_params=pltpu.CompilerParams(
            dimension_semantics=("arbitrary",), vmem_limit_bytes=VMEM_LIMIT),
        name="cast",
    )(w)


def _regroup_kernel(w_ref, o_ref, *, groups):
    c = 0
    for a, b in groups:
        o_ref[:, c:c + b - a] = w_ref[:, a:b].astype(o_ref.dtype)
        c += b - a
    o_ref[:, c:] = jnp.zeros((o_ref.shape[0], o_ref.shape[1] - c), o_ref.dtype)


def _regrouped_bf16(w, layer, groups, n_out):
    _, r, c = w.shape
    tr = ROWS_CAST
    return pl.pallas_call(
        functools.partial(_regroup_kernel, groups=groups),
        grid=(r // tr,),
        in_specs=[pl.BlockSpec((None, tr, c), lambda i: (layer, i, 0))],
        out_specs=pl.BlockSpec((tr, n_out), lambda i: (i, 0)),
        out_shape=jax.ShapeDtypeStruct((r, n_out), BF16),
        compiler_params=pltpu.CompilerParams(
            dimension_semantics=("arbitrary",), vmem_limit_bytes=VMEM_LIMIT),
        name="regroup",
    )(w)


def _block_diag(w, per_tile):
    nb, e, _ = w.shape
    w = w.reshape(nb // per_tile, per_tile, e, e)
    eye = jnp.eye(per_tile, dtype=w.dtype)
    out = jnp.einsum("tpij,pq->tpiqj", w, eye)
    return out.reshape(nb // per_tile, per_tile * e, per_tile * e)


def _to_segment_order(x):
    b, s, d = x.shape
    x = x.reshape(b, s // PERM, SUBLANES, PERM // SUBLANES, d)
    return jnp.swapaxes(x, 2, 3).reshape(b * s, d)


def _from_segment_order(x2, b, s):
    d = x2.shape[1]
    x = x2.reshape(b, s // PERM, PERM // SUBLANES, SUBLANES, d)
    return jnp.swapaxes(x, 2, 3).reshape(b, s, d)


def kernel(x, norm_mix_pre, norm_mix_post, norm_ffn_pre, norm_ffn_post, w_in, conv_m_w, conv_m_b, w_q_m, w_k_m, w_v_m, b_i_m, b_f_m, norm_m, conv_r_w, conv_r_b, w_a_r, b_a_r, w_x_r, b_x_r, lam_r, conv_s_w, conv_s_b, dt_bias_s, a_log_s, d_skip_s, norm_s, w_out, w_up, conv_f_w, conv_f_b, w_down):
    batch, seq, d = x.shape
    depth = w_in.shape[0]
    dm = conv_m_w.shape[2]
    dr = conv_r_w.shape[2]
    ds = norm_s.shape[1]
    dconv = conv_s_w.shape[2]
    hm = b_i_m.shape[1]
    hs = dt_bias_s.shape[1]
    sizes = (dm, dm, hm, hm, dr, dr, ds, dconv, hs)
    offs = [0]
    for s in sizes:
        offs.append(offs[-1] + s)
    col = lambda j: (offs[j], offs[j + 1])
    main_cols = [col(0), col(4), col(7), col(1), col(5), col(6)]
    n_main = sum(b - a for a, b in main_cols)

    row = lambda v: v.reshape(1, -1).astype(F32)
    x2 = _to_segment_order(x)
    for l in range(depth):
        w_all = _regrouped_bf16(w_in, l, tuple(main_cols) + ((offs[2], offs[4]), (offs[8], offs[9])),
                                n_main + LANES)
        gate_b = jnp.concatenate([b_i_m[l], b_f_m[l], dt_bias_s[l],
                                  jnp.zeros((LANES - 2 * hm - hs,), F32)]).reshape(1, LANES)
        alog = jnp.concatenate([jnp.zeros((GATE_DT,), F32), a_log_s[l],
                                jnp.zeros((LANES - GATE_DT - hs,), F32)]).reshape(1, LANES)
        cw = jnp.concatenate([conv_m_w[l], conv_r_w[l], conv_s_w[l]], axis=1)
        cb = jnp.concatenate([conv_m_b[l], conv_r_b[l], conv_s_b[l]]).reshape(1, -1)

        x1, h2 = _front(x2, row(norm_mix_pre[l]), w_all, cw, cb, jnp.concatenate([w_q_m[l], w_k_m[l]], axis=-1).astype(BF16),
                        w_v_m[l].astype(BF16), gate_b, row(norm_m[l]), _block_diag(w_a_r[l], 4).astype(BF16),
                        _block_diag(w_x_r[l], 4).astype(BF16), row(b_a_r[l]), row(b_x_r[l]), row(lam_r[l]),
                        alog, row(jnp.repeat(d_skip_s[l], SSD_HEAD_DIM)), row(norm_s[l]),
                        _layer_bf16(w_out, l), row(norm_mix_post[l]), row(norm_ffn_pre[l]),
                        batch=batch, seq=seq, n_main=n_main)
        x2 = _ffn(h2, x1, _layer_bf16(w_up, l), conv_f_w[l], row(conv_f_b[l]),
                  _layer_bf16(w_down, l), row(norm_ffn_post[l]), batch=batch, seq=seq)
    return _from_segment_order(x2, batch, seq)
```
